```python
import numpy as np
import jax, jax.numpy as jnp
from jax import lax

D_MODEL = 1024
BATCH = 2
SEQ = 16384
DEPTH = 4

CHUNK = 64
N_MIXERS = 4
HEAD_DIM = 64
MIX_WIDTH = D_MODEL
MEM_LEN = 256
X_HEADS = 4
X_WIDTH = X_HEADS * HEAD_DIM
BRANCH_WIDTH = MIX_WIDTH + X_WIDTH
SG_BLOCK = 128
SG_GROUPS = 8
SG_GROUP_CH = MIX_WIDTH // SG_GROUPS
SWA_WINDOW = 128
SWA_Q_HEADS = MIX_WIDTH // HEAD_DIM
SWA_KV_HEADS = 2
SWA_GROUP = SWA_Q_HEADS // SWA_KV_HEADS
WINDOW_CHUNKS = SWA_WINDOW // CHUNK
BAND = (WINDOW_CHUNKS + 1) * CHUNK
REL_BUCKETS = 32
REL_MAX_DIST = 128
CONV_WIDTH = 31
SHORT_CONV_WIDTH = 3
D_FF = -(-8 * D_MODEL // (3 * 256)) * 256
DEEPNORM_ALPHA = (2 * DEPTH) ** 0.25
DEEPNORM_BETA = (8 * DEPTH) ** -0.25
LN_EPS = 1e-5
NEG_INF = -1e30
N_PER_TYPE = tuple(len(range(m, DEPTH, N_MIXERS)) for m in range(N_MIXERS))

kernel_name = 'hybrid_chunk_causal_interleaved_trunk'


def layer_norm(x, g, b):
    xf = x.astype(jnp.float32)
    mu = jnp.mean(xf, -1, keepdims=True)
    var = jnp.mean(jnp.square(xf - mu), -1, keepdims=True)
    y = (xf - mu) * lax.rsqrt(var + LN_EPS)
    return (y * g.astype(jnp.float32) + b.astype(jnp.float32)).astype(x.dtype)


def causal_depthwise_conv(x, w):
    k = w.shape[0]
    return lax.conv_general_dilated(
        x, w[:, None, :].astype(x.dtype), window_strides=(1,), padding=[(k - 1, 0)],
        dimension_numbers=('NWC', 'WIO', 'NWC'), feature_group_count=x.shape[-1])


def t5_bucket(rel):
    nb = REL_BUCKETS // 2
    ret = (rel > 0).astype(np.int32) * nb
    n = np.abs(rel)
    max_exact = nb // 2
    large = max_exact + (np.log(np.maximum(n, 1) / max_exact)
                         / np.log(REL_MAX_DIST / max_exact) * (nb - max_exact)).astype(np.int32)
    large = np.minimum(large, nb - 1)
    return (ret + np.where(n < max_exact, n, large)).astype(np.int32)


def spatial_gating_mixer(x, w_in, v_g, v_b, w_s, b_s):
    bsz, s, _ = x.shape
    h = x @ w_in
    z = jax.nn.gelu(h[..., :2 * MIX_WIDTH])
    qx = h[..., 2 * MIX_WIDTH:]
    u = z[..., :MIX_WIDTH]
    v = layer_norm(z[..., MIX_WIDTH:], v_g, v_b)
    pos = np.arange(SG_BLOCK) // CHUNK
    mask = pos[None, :] <= pos[:, None]
    w_m = jnp.where(mask[None], w_s, 0.0)
    v = v.reshape(bsz, s // SG_BLOCK, SG_BLOCK, SG_GROUPS, SG_GROUP_CH)
    sv = jnp.einsum('gij,bnjgc->bnigc', w_m, v) + b_s.T[:, :, None]
    return u * sv.reshape(bsz, s, MIX_WIDTH), qx


def swa_sink_mixer(x, w_in, sinks, rel_bias):
    bsz, s, _ = x.shape
    n_chunks = s // CHUNK
    kv_w = SWA_KV_HEADS * HEAD_DIM
    h = x @ w_in
    q = h[..., :MIX_WIDTH].reshape(bsz, n_chunks, CHUNK, SWA_KV_HEADS, SWA_GROUP, HEAD_DIM)
    k = h[..., MIX_WIDTH:MIX_WIDTH + kv_w]
    v = h[..., MIX_WIDTH + kv_w:MIX_WIDTH + 2 * kv_w]
    qx = h[..., MIX_WIDTH + 2 * kv_w:]

    def band(t):
        t = jnp.pad(t, ((0, 0), (WINDOW_CHUNKS * CHUNK, 0), (0, 0)))
        t = t.reshape(bsz, n_chunks + WINDOW_CHUNKS, CHUNK, SWA_KV_HEADS, HEAD_DIM)
        return jnp.concatenate([t[:, w:w + n_chunks] for w in range(WINDOW_CHUNKS + 1)], axis=2)

    kb, vb = band(k), band(v)
    scores = jnp.einsum('bnqkgd,bnskd->bnkgqs', q, kb).astype(jnp.float32) * (HEAD_DIM ** -0.5)
    qpos = np.arange(CHUNK)[:, None]
    kpos = np.arange(BAND)[None, :] - WINDOW_CHUNKS * CHUNK
    buckets = t5_bucket(kpos - qpos)
    bias = rel_bias.astype(jnp.float32)[buckets]
    bias = bias.transpose(2, 0, 1).reshape(SWA_KV_HEADS, SWA_GROUP, CHUNK, BAND)
    valid = (np.arange(n_chunks)[:, None] + np.arange(BAND)[None, :] // CHUNK - WINDOW_CHUNKS) >= 0
    scores = jnp.where(valid[None, :, None, None, None, :], scores + bias, NEG_INF)
    sink = jnp.broadcast_to(sinks.astype(jnp.float32).reshape(SWA_KV_HEADS, SWA_GROUP, 1, 1),
                            scores.shape[:-1] + (1,))
    probs = jax.nn.softmax(jnp.concatenate([scores, sink], axis=-1), axis=-1)[..., :BAND]
    out = jnp.einsum('bnkgqs,bnskd->bnqkgd', probs.astype(x.dtype), vb)
    return out.reshape(bsz, s, MIX_WIDTH), qx


def conformer_conv_mixer(x, w_in, conv_w, conv_b, ln_g, ln_b):
    h = x @ w_in
    a, g = h[..., :MIX_WIDTH], h[..., MIX_WIDTH:2 * MIX_WIDTH]
    qx = h[..., 2 * MIX_WIDTH:]
    y = causal_depthwise_conv(a * jax.nn.sigmoid(g), conv_w) + conv_b
    return jax.nn.silu(layer_norm(y, ln_g, ln_b)), qx


def short_conv_mixer(x, w_in, conv_w):
    h = x @ w_in
    bg = h[..., :MIX_WIDTH]
    cg = h[..., MIX_WIDTH:2 * MIX_WIDTH]
    hv = h[..., 2 * MIX_WIDTH:3 * MIX_WIDTH]
    qx = h[..., 3 * MIX_WIDTH:]
    return bg * causal_depthwise_conv(cg * hv, conv_w), qx


def memory_cross_attention(qx, mem, w_mem_kv):
    bsz, s, _ = qx.shape
    q = qx.reshape(bsz, s, X_HEADS, HEAD_DIM)
    kv = (mem @ w_mem_kv).reshape(bsz, mem.shape[1], 2, X_HEADS, HEAD_DIM)
    sc = jnp.einsum('bshd,bmhd->bhsm', q, kv[:, :, 0]).astype(jnp.float32) * (HEAD_DIM ** -0.5)
    p = jax.nn.softmax(sc, axis=-1).astype(qx.dtype)
    return jnp.einsum('bhsm,bmhd->bshd', p, kv[:, :, 1]).reshape(bsz, s, X_WIDTH)


def swiglu_ffn(x, w_in, w_down):
    h = x @ w_in
    return (jax.nn.silu(h[..., :D_FF]) * h[..., D_FF:]) @ w_down


def setup_inputs(seed: int = 0) -> dict:
    key = jax.random.key(seed)
    ks = iter(jax.random.split(key, 48))

    def nrm(shape, scale):
        return jax.random.normal(next(ks), shape, jnp.float32) * scale

    n_a, n_b, n_c, n_d = N_PER_TYPE
    d = D_MODEL
    return {
        'x': nrm((BATCH, SEQ, d), 1.0),
        'mem': nrm((BATCH, MEM_LEN, d), 1.0),
        'a_w_in': nrm((n_a, d, 2 * MIX_WIDTH + X_WIDTH), d ** -0.5),
        'a_v_ln_g': 1.0 + nrm((n_a, MIX_WIDTH), 0.02),
        'a_v_ln_b': nrm((n_a, MIX_WIDTH), 0.02),
        'a_w_s': nrm((n_a, SG_GROUPS, SG_BLOCK, SG_BLOCK), SG_BLOCK ** -0.5),
        'a_b_s': 1.0 + nrm((n_a, SG_GROUPS, SG_BLOCK), 0.02),
        'b_w_in': nrm((n_b, d, MIX_WIDTH + 2 * SWA_KV_HEADS * HEAD_DIM + X_WIDTH), d ** -0.5),
        'b_sinks': nrm((n_b, SWA_Q_HEADS), 0.5),
        'rel_bias': nrm((REL_BUCKETS, SWA_Q_HEADS), 0.1),
        'c_w_in': nrm((n_c, d, 2 * MIX_WIDTH + X_WIDTH), d ** -0.5),
        'c_conv_w': nrm((n_c, CONV_WIDTH, MIX_WIDTH), CONV_WIDTH ** -0.5),
        'c_conv_b': nrm((n_c, MIX_WIDTH), 0.02),
        'c_ln_g': 1.0 + nrm((n_c, MIX_WIDTH), 0.02),
        'c_ln_b': nrm((n_c, MIX_WIDTH), 0.02),
        'd_w_in': nrm((n_d, d, 3 * MIX_WIDTH + X_WIDTH), d ** -0.5),
        'd_conv_w': nrm((n_d, SHORT_CONV_WIDTH, MIX_WIDTH), SHORT_CONV_WIDTH ** -0.5),
        'w_mem_kv': nrm((DEPTH, d, 2 * X_WIDTH), d ** -0.5),
        'w_o': nrm((DEPTH, BRANCH_WIDTH, d), BRANCH_WIDTH ** -0.5 * DEEPNORM_BETA),
        'ln1_g': 1.0 + nrm((DEPTH, d), 0.02),
        'ln1_b': nrm((DEPTH, d), 0.02),
        'ffn_w_in': nrm((DEPTH, d, 2 * D_FF), d ** -0.5),
        'ffn_w_down': nrm((DEPTH, D_FF, d), D_FF ** -0.5 * DEEPNORM_BETA),
        'ln2_g': 1.0 + nrm((DEPTH, d), 0.02),
        'ln2_b': nrm((DEPTH, d), 0.02),
    }


def reference(x, mem, a_w_in, a_v_ln_g, a_v_ln_b, a_w_s, a_b_s, b_w_in, b_sinks, rel_bias,
              c_w_in, c_conv_w, c_conv_b, c_ln_g, c_ln_b, d_w_in, d_conv_w, w_mem_kv, w_o,
              ln1_g, ln1_b, ffn_w_in, ffn_w_down, ln2_g, ln2_b):
    for i in range(DEPTH):
        m, j = i % N_MIXERS, i // N_MIXERS
        if m == 0:
            mix, qx = spatial_gating_mixer(x, a_w_in[j], a_v_ln_g[j], a_v_ln_b[j], a_w_s[j], a_b_s[j])
        elif m == 1:
            mix, qx = swa_sink_mixer(x, b_w_in[j], b_sinks[j], rel_bias)
        elif m == 2:
            mix, qx = conformer_conv_mixer(x, c_w_in[j], c_conv_w[j], c_conv_b[j], c_ln_g[j], c_ln_b[j])
        else:
            mix, qx = short_conv_mixer(x, d_w_in[j], d_conv_w[j])
        branch = jnp.concatenate([mix, memory_cross_attention(qx, mem, w_mem_kv[i])], axis=-1)
        x = layer_norm(DEEPNORM_ALPHA * x + branch @ w_o[i], ln1_g[i], ln1_b[i])
        x = layer_norm(DEEPNORM_ALPHA * x + swiglu_ffn(x, ffn_w_in[i], ffn_w_down[i]), ln2_g[i], ln2_b[i])
    return x
```

```python
import functools

import numpy as np
import jax
import jax.numpy as jnp
from jax import lax
from jax.experimental import pallas as pl
from jax.experimental.pallas import tpu as pltpu

D_MODEL = 1024
DEPTH = 4
CHUNK = 64
HEAD_DIM = 64
MIX_WIDTH = D_MODEL
MEM_LEN = 256
X_HEADS = 4
X_WIDTH = X_HEADS * HEAD_DIM
BRANCH_WIDTH = MIX_WIDTH + X_WIDTH
SG_BLOCK = 128
SG_GROUPS = 8
SWA_KV_HEADS = 2
SWA_GROUP = 8
BAND = 192
BAND_PAD = 256
REL_BUCKETS = 32
REL_MAX_DIST = 128
CONV_WIDTH = 31
SHORT_CONV_WIDTH = 3
D_FF = 2816
FF_CHUNK = 256
N_FF_CHUNKS = D_FF // FF_CHUNK
DEEPNORM_ALPHA = (2 * DEPTH) ** 0.25
LN_EPS = 1e-5
NEG_INF = -1e30
QK_SCALE = HEAD_DIM ** -0.5

TILE = 512
CONV_HALO = 32
SHORT_HALO = 8
VMEM_LIMIT_BYTES = 56 * 1024 * 1024

BF16 = jnp.bfloat16
F32 = jnp.float32


def _dot(a, b):
    return jnp.dot(a, b, preferred_element_type=F32)


def _layer_norm(y, g, b):
    mu = jnp.mean(y, -1, keepdims=True)
    d = y - mu
    var = jnp.mean(d * d, -1, keepdims=True)
    return d * lax.rsqrt(var + LN_EPS) * g + b


def _t5_bucket(rel):
    nb = REL_BUCKETS // 2
    ret = (rel > 0).astype(np.int32) * nb
    n = np.abs(rel)
    max_exact = nb // 2
    large = max_exact + (np.log(np.maximum(n, 1) / max_exact)
                         / np.log(REL_MAX_DIST / max_exact) * (nb - max_exact)).astype(np.int32)
    large = np.minimum(large, nb - 1)
    return (ret + np.where(n < max_exact, n, large)).astype(np.int32)


def _kv_kernel(mem_ref, w_ref, kbd_ref, vbd_ref):
    kv = _dot(mem_ref[0].astype(BF16), w_ref[0].astype(BF16))
    kt = (kv[:, :X_WIDTH] * QK_SCALE).T
    v = kv[:, X_WIDTH:]
    row_head = lax.broadcasted_iota(jnp.int32, (X_WIDTH, MEM_LEN), 0) // HEAD_DIM
    col_head = lax.broadcasted_iota(jnp.int32, (MEM_LEN, X_WIDTH), 1) // HEAD_DIM
    for h in range(X_HEADS):
        kbd_ref[0, 0, :, h * MEM_LEN:(h + 1) * MEM_LEN] = jnp.where(row_head == h, kt, 0.0).astype(BF16)
        vbd_ref[0, 0, h * MEM_LEN:(h + 1) * MEM_LEN, :] = jnp.where(col_head == h, v, 0.0).astype(BF16)


def _memory_kv(mem, w_mem_kv):
    bsz = mem.shape[0]
    return pl.pallas_call(
        _kv_kernel,
        grid=(DEPTH, bsz),
        in_specs=[pl.BlockSpec((1, MEM_LEN, D_MODEL), lambda l, b: (b, 0, 0)),
                  pl.BlockSpec((1, D_MODEL, 2 * X_WIDTH), lambda l, b: (l, 0, 0))],
        out_specs=[pl.BlockSpec((1, 1, X_WIDTH, X_HEADS * MEM_LEN), lambda l, b: (l, b, 0, 0)),
                   pl.BlockSpec((1, 1, X_HEADS * MEM_LEN, X_WIDTH), lambda l, b: (l, b, 0, 0))],
        out_shape=[jax.ShapeDtypeStruct((DEPTH, bsz, X_WIDTH, X_HEADS * MEM_LEN), BF16),
                   jax.ShapeDtypeStruct((DEPTH, bsz, X_HEADS * MEM_LEN, X_WIDTH), BF16)],
        name="memory_kv",
    )(mem, w_mem_kv)


def _bias_kernel(rb_ref, bucket_ref, out_ref):
    h = pl.program_id(0)
    bk = bucket_ref[...]
    acc = jnp.full(bk.shape, NEG_INF, F32)
    for b in range(REL_BUCKETS):
        acc = jnp.where(bk == b, rb_ref[b, h], acc)
    out_ref[0] = acc


def _bias_table(rel_bias):
    n_heads = rel_bias.shape[1]
    qpos = np.arange(CHUNK)[:, None]
    kpos = np.arange(BAND)[None, :] - (BAND - CHUNK)
    buckets = np.full((CHUNK, BAND_PAD), -1, np.int32)
    buckets[:, :BAND] = _t5_bucket(kpos - qpos)
    return pl.pallas_call(
        _bias_kernel,
        grid=(n_heads,),
        in_specs=[pl.BlockSpec(memory_space=pltpu.SMEM),
                  pl.BlockSpec((CHUNK, BAND_PAD), lambda h: (0, 0))],
        out_specs=pl.BlockSpec((1, CHUNK, BAND_PAD), lambda h: (h, 0, 0)),
        out_shape=jax.ShapeDtypeStruct((n_heads, CHUNK, BAND_PAD), F32),
        name="rel_bias_table",
    )(rel_bias, jnp.asarray(buckets))


def _cross_attention(qx, kbd_ref, vbd_ref):
    s = _dot(qx.astype(BF16), kbd_ref[0])
    probs = []
    for h in range(X_HEADS):
        sh = s[:, h * MEM_LEN:(h + 1) * MEM_LEN]
        e = jnp.exp(sh - jnp.max(sh, -1, keepdims=True))
        probs.append((e / jnp.sum(e, -1, keepdims=True)).astype(BF16))
    return _dot(jnp.concatenate(probs, -1), vbd_ref[0])


def _project_and_norm(x, br_ref, wo_ref, g_ref, b_ref, out_ref):
    y = DEEPNORM_ALPHA * x + _dot(br_ref[...], wo_ref[...])
    out_ref[0] = _layer_norm(y, g_ref[...], b_ref[...])


def _sg_kernel(x_ref, win_ref, vg_ref, vb_ref, ws_ref, bs_ref, kbd_ref, vbd_ref, wo_ref, g_ref, b_ref,
               out_ref, br_ref, u_ref, v_ref):
    x = x_ref[0]
    xb = x.astype(BF16)
    u_ref[...] = jax.nn.gelu(_dot(xb, win_ref[:, :MIX_WIDTH]))
    zv = jax.nn.gelu(_dot(xb, win_ref[:, MIX_WIDTH:2 * MIX_WIDTH]))
    v_ref[...] = _layer_norm(zv, vg_ref[...], vb_ref[...]).astype(BF16)
    qx = _dot(xb, win_ref[:, 2 * MIX_WIDTH:])
    br_ref[:, MIX_WIDTH:] = _cross_attention(qx, kbd_ref, vbd_ref).astype(BF16)

    row_chunk = lax.broadcasted_iota(jnp.int32, (SG_BLOCK, SG_BLOCK), 0) // CHUNK
    col_chunk = lax.broadcasted_iota(jnp.int32, (SG_BLOCK, SG_BLOCK), 1) // CHUNK
    gch = MIX_WIDTH // SG_GROUPS
    for g in range(SG_GROUPS):
        w_m = jnp.where(col_chunk <= row_chunk, ws_ref[g], 0.0).astype(BF16)
        bias = bs_ref[g]
        for n in range(TILE // SG_BLOCK):
            rows = slice(n * SG_BLOCK, (n + 1) * SG_BLOCK)
            cols = slice(g * gch, (g + 1) * gch)
            sv = _dot(w_m, v_ref[rows, cols]) + bias
            br_ref[rows, cols] = (u_ref[rows, cols] * sv).astype(BF16)
    _project_and_norm(x, br_ref, wo_ref, g_ref, b_ref, out_ref)


def _swa_kernel(x_ref, win_ref, bias_ref, sink_ref, kbd_ref, vbd_ref, wo_ref, g_ref, b_ref,
                out_ref, br_ref, q_ref, k_ref, v_ref):
    s_idx = pl.program_id(1)
    halo = BAND - CHUNK
    kv_w = SWA_KV_HEADS * HEAD_DIM

    @pl.when(s_idx == 0)
    def _():
        k_ref[...] = jnp.zeros(k_ref.shape, BF16)
        v_ref[...] = jnp.zeros(v_ref.shape, BF16)

    @pl.when(s_idx != 0)
    def _():
        k_ref[:, 0:halo, :] = k_ref[:, TILE:TILE + halo, :]
        v_ref[:, 0:halo, :] = v_ref[:, TILE:TILE + halo, :]

    x = x_ref[0]
    xb = x.astype(BF16)
    q_ref[...] = (_dot(xb, win_ref[:, :MIX_WIDTH]) * QK_SCALE).astype(BF16)
    kvx = _dot(xb, win_ref[:, MIX_WIDTH:])
    lane = lax.broadcasted_iota(jnp.int32, (TILE, kv_w), 1)
    low = lane < HEAD_DIM
    for name_ref, t in ((k_ref, kvx[:, :kv_w]), (v_ref, kvx[:, kv_w:2 * kv_w])):
        swapped = pltpu.roll(t, HEAD_DIM, 1)
        name_ref[0, halo:halo + TILE, :] = jnp.where(low, t, swapped).astype(BF16)
        name_ref[1, halo:halo + TILE, :] = jnp.where(low, swapped, t).astype(BF16)
    br_ref[:, MIX_WIDTH:] = _cross_attention(kvx[:, 2 * kv_w:], kbd_ref, vbd_ref).astype(BF16)

    rows_per_kv = SWA_GROUP * CHUNK
    col = lax.broadcasted_iota(jnp.int32, (rows_per_kv, BAND_PAD), 1)
    lane_q = lax.broadcasted_iota(jnp.int32, (CHUNK, 2 * HEAD_DIM), 1)
    low_q = lane_q < HEAD_DIM

    def chunk_body(c, carry):
        r0 = pl.multiple_of(c * CHUNK, CHUNK)
        n_invalid = jnp.maximum(0, 2 - (s_idx * (TILE // CHUNK) + c)) * CHUNK
        for kh in range(SWA_KV_HEADS):
            pieces = []
            for j in range(SWA_GROUP // 2):
                qp = q_ref[pl.ds(r0, CHUNK), kh * 512 + j * 128:kh * 512 + (j + 1) * 128]
                pieces.append(jnp.where(low_q, qp, jnp.zeros_like(qp)))
                pieces.append(jnp.where(low_q, jnp.zeros_like(qp), qp))
            qs = jnp.concatenate(pieces, 0)
            kb = k_ref[kh, pl.ds(r0, BAND_PAD), :]
            vb = v_ref[kh, pl.ds(r0, BAND_PAD), :]
            sc = lax.dot_general(qs, kb, (((1,), (1,)), ((), ())), preferred_element_type=F32)
            sc = jnp.where(col < n_invalid, NEG_INF, sc + bias_ref[kh])
            sink = sink_ref[kh]
            m = jnp.maximum(jnp.max(sc, -1, keepdims=True), sink)
            e = jnp.exp(sc - m)
            denom = jnp.sum(e, -1, keepdims=True) + jnp.exp(sink - m)
            o = _dot((e / denom).astype(BF16), vb)
            for j in range(SWA_GROUP // 2):
                lo = o[(2 * j) * CHUNK:(2 * j + 1) * CHUNK]
                hi = o[(2 * j + 1) * CHUNK:(2 * j + 2) * CHUNK]
                br_ref[pl.ds(r0, CHUNK), kh * 512 + j * 128:kh * 512 + (j + 1) * 128] = (
                    jnp.where(low_q, lo, hi).astype(BF16))
        return carry

    lax.fori_loop(0, TILE // CHUNK, chunk_body, 0)
    _project_and_norm(x, br_ref, wo_ref, g_ref, b_ref, out_ref)


def _causal_conv(ext_ref, w_ref, halo, width, store):
    row_block = 128
    for cb in range(MIX_WIDTH // 128):
        cols = slice(cb * 128, (cb + 1) * 128)
        for rb in range(TILE // row_block):
            acc = jnp.zeros((row_block, 128), F32)
            for k in range(width):
                start = rb * row_block + halo - (width - 1) + k
                acc = acc + w_ref[k:k + 1, cols] * ext_ref[start:start + row_block, cols]
            store(slice(rb * row_block, (rb + 1) * row_block), cols, acc)


def _conf_kernel(x_ref, win_ref, cw_ref, cb_ref, lg_ref, lb_ref, kbd_ref, vbd_ref, wo_ref, g_ref, b_ref,
                 out_ref, br_ref, ext_ref, y_ref):
    s_idx = pl.program_id(1)

    @pl.when(s_idx == 0)
    def _():
        ext_ref[0:CONV_HALO, :] = jnp.zeros((CONV_HALO, MIX_WIDTH), F32)

    @pl.when(s_idx != 0)
    def _():
        ext_ref[0:CONV_HALO, :] = ext_ref[TILE:TILE + CONV_HALO, :]

    x = x_ref[0]
    xb = x.astype(BF16)
    for c in range(MIX_WIDTH // 256):
        cols = slice(c * 256, (c + 1) * 256)
        a = _dot(xb, win_ref[:, c * 256:(c + 1) * 256])
        gate = _dot(xb, win_ref[:, MIX_WIDTH + c * 256:MIX_WIDTH + (c + 1) * 256])
        ext_ref[CONV_HALO:CONV_HALO + TILE, cols] = a * jax.nn.sigmoid(gate)
    qx = _dot(xb, win_ref[:, 2 * MIX_WIDTH:])
    br_ref[:, MIX_WIDTH:] = _cross_attention(qx, kbd_ref, vbd_ref).astype(BF16)

    def store(rows, cols, acc):
        y_ref[rows, cols] = acc + cb_ref[:, cols]

    _causal_conv(ext_ref, cw_ref, CONV_HALO, CONV_WIDTH, store)
    br_ref[:, :MIX_WIDTH] = jax.nn.silu(_layer_norm(y_ref[...], lg_ref[...], lb_ref[...])).astype(BF16)
    _project_and_norm(x, br_ref, wo_ref, g_ref, b_ref, out_ref)


def _short_kernel(x_ref, win_ref, cw_ref, kbd_ref, vbd_ref, wo_ref, g_ref, b_ref,
                  out_ref, br_ref, ext_ref, bg_ref):
    s_idx = pl.program_id(1)

    @pl.when(s_idx == 0)
    def _():
        ext_ref[0:SHORT_HALO, :] = jnp.zeros((SHORT_HALO, MIX_WIDTH), F32)

    @pl.when(s_idx != 0)
    def _():
        ext_ref[0:SHORT_HALO, :] = ext_ref[TILE:TILE + SHORT_HALO, :]

    x = x_ref[0]
    xb = x.astype(BF16)
    for c in range(MIX_WIDTH // 256):
        cols = slice(c * 256, (c + 1) * 256)
        bg_ref[:, cols] = _dot(xb, win_ref[:, c * 256:(c + 1) * 256])
        cg = _dot(xb, win_ref[:, MIX_WIDTH + c * 256:MIX_WIDTH + (c + 1) * 256])
        hv = _dot(xb, win_ref[:, 2 * MIX_WIDTH + c * 256:2 * MIX_WIDTH + (c + 1) * 256])
        ext_ref[SHORT_HALO:SHORT_HALO + TILE, cols] = cg * hv
    qx = _dot(xb, win_ref[:, 3 * MIX_WIDTH:])
    br_ref[:, MIX_WIDTH:] = _cross_attention(qx, kbd_ref, vbd_ref).astype(BF16)

    def store(rows, cols, acc):
        br_ref[rows, cols] = (bg_ref[rows, cols] * acc).astype(BF16)

    _causal_conv(ext_ref, cw_ref, SHORT_HALO, SHORT_CONV_WIDTH, store)
    _project_and_norm(x, br_ref, wo_ref, g_ref, b_ref, out_ref)


def _ffn_kernel(x_ref, win_ref, wd_ref, g_ref, b_ref, out_ref, act_ref):
    x = x_ref[0]
    xb = x.astype(BF16)
    for c in range(N_FF_CHUNKS):
        h = _dot(xb, win_ref[c])
        act_ref[:, c * FF_CHUNK:(c + 1) * FF_CHUNK] = (
            jax.nn.silu(h[:, :FF_CHUNK]) * h[:, FF_CHUNK:]).astype(BF16)
    y = DEEPNORM_ALPHA * x + _dot(act_ref[...], wd_ref[...])
    out_ref[0] = _layer_norm(y, g_ref[...], b_ref[...])


def _const_spec(shape):
    zeros = (0,) * len(shape)
    return pl.BlockSpec(shape, lambda b, s: zeros)


def _per_batch_spec(shape):
    zeros = (0,) * (len(shape) - 1)
    return pl.BlockSpec((1,) + tuple(shape[1:]), lambda b, s: (b,) + zeros)


def _token_call(body, name, x, consts, per_batch, scratch):
    bsz, seq, _ = x.shape
    x_spec = pl.BlockSpec((1, TILE, D_MODEL), lambda b, s: (b, s, 0))
    return pl.pallas_call(
        body,
        grid=(bsz, seq // TILE),
        in_specs=([x_spec] + [_const_spec(c.shape) for c in consts[0]]
                  + [_per_batch_spec(p.shape) for p in per_batch]
                  + [_const_spec(c.shape) for c in consts[1]]),
        out_specs=x_spec,
        out_shape=jax.ShapeDtypeStruct(x.shape, F32),
        scratch_shapes=scratch,
        compiler_params=pltpu.CompilerParams(
            dimension_semantics=("arbitrary", "arbitrary"),
            vmem_limit_bytes=VMEM_LIMIT_BYTES),
        name=name,
    )(x, *consts[0], *per_batch, *consts[1])


def _row(v):
    return v.reshape(1, -1)


def kernel(x, mem, a_w_in, a_v_ln_g, a_v_ln_b, a_w_s, a_b_s, b_w_in, b_sinks, rel_bias, c_w_in, c_conv_w, c_conv_b, c_ln_g, c_ln_b, d_w_in, d_conv_w, w_mem_kv, w_o, ln1_g, ln1_b, ffn_w_in, ffn_w_down, ln2_g, ln2_b):
    kbd, vbd = _memory_kv(mem, w_mem_kv)
    branch = pltpu.VMEM((TILE, BRANCH_WIDTH), BF16)
    for i in range(DEPTH):
        m, j = i % 4, i // 4
        tail = [w_o[i].astype(BF16), _row(ln1_g[i]), _row(ln1_b[i])]
        kv = [kbd[i], vbd[i]]
        if m == 0:
            bias = jnp.broadcast_to(a_b_s[j][:, :, None], (SG_GROUPS, SG_BLOCK, SG_BLOCK))
            head = [a_w_in[j].astype(BF16), _row(a_v_ln_g[j]), _row(a_v_ln_b[j]), a_w_s[j], bias]
            scratch = [branch, pltpu.VMEM((TILE, MIX_WIDTH), F32), pltpu.VMEM((TILE, MIX_WIDTH), BF16)]
            x = _token_call(_sg_kernel, "mixer_spatial_gating", x, (head, tail), kv, scratch)
        elif m == 1:
            table = _bias_table(rel_bias).reshape(SWA_KV_HEADS, SWA_GROUP * CHUNK, BAND_PAD)
            sinks = jnp.repeat(b_sinks[j], CHUNK).reshape(SWA_KV_HEADS, SWA_GROUP * CHUNK, 1)
            head = [b_w_in[j].astype(BF16), table, sinks]
            ext = pltpu.VMEM((SWA_KV_HEADS, TILE + BAND, SWA_KV_HEADS * HEAD_DIM), BF16)
            scratch = [branch, pltpu.VMEM((TILE, MIX_WIDTH), BF16), ext, ext]
            x = _token_call(_swa_kernel, "mixer_swa", x, (head, tail), kv, scratch)
        elif m == 2:
            head = [c_w_in[j].astype(BF16), c_conv_w[j], _row(c_conv_b[j]), _row(c_ln_g[j]), _row(c_ln_b[j])]
            scratch = [branch, pltpu.VMEM((TILE + CONV_HALO, MIX_WIDTH), F32), pltpu.VMEM((TILE, MIX_WIDTH), F32)]
            x = _token_call(_conf_kernel, "mixer_conformer", x, (head, tail), kv, scratch)
        else:
            head = [d_w_in[j].astype(BF16), d_conv_w[j]]
            scratch = [branch, pltpu.VMEM((TILE + SHORT_HALO, MIX_WIDTH), F32), pltpu.VMEM((TILE, MIX_WIDTH), F32)]
            x = _token_call(_short_kernel, "mixer_short_conv", x, (head, tail), kv, scratch)

        w_in = ffn_w_in[i].astype(BF16)
        w_cat = jnp.concatenate([w_in[:, :D_FF].reshape(D_MODEL, N_FF_CHUNKS, FF_CHUNK),
                                 w_in[:, D_FF:].reshape(D_MODEL, N_FF_CHUNKS, FF_CHUNK)], -1)
        w_cat = w_cat.transpose(1, 0, 2)
        head = [w_cat, ffn_w_down[i].astype(BF16), _row(ln2_g[i]), _row(ln2_b[i])]
        x = _token_call(_ffn_kernel, "ffn_swiglu", x, (head, []), [], [pltpu.VMEM((TILE, D_FF), BF16)])
    return x
```

```python
import functools

import numpy as np
import jax
import jax.numpy as jnp
from jax import lax
from jax.experimental import pallas as pl
from jax.experimental.pallas import tpu as pltpu

D_MODEL = 1024
DEPTH = 4
CHUNK = 64
HEAD_DIM = 64
MIX_WIDTH = D_MODEL
MEM_LEN = 256
X_HEADS = 4
X_WIDTH = X_HEADS * HEAD_DIM
BRANCH_WIDTH = MIX_WIDTH + X_WIDTH
SG_BLOCK = 128
SG_GROUPS = 8
SWA_KV_HEADS = 2
SWA_GROUP = 8
BAND = 192
BAND_PAD = 256
REL_BUCKETS = 32
REL_MAX_DIST = 128
CONV_WIDTH = 31
SHORT_CONV_WIDTH = 3
D_FF = 2816
FF_CHUNK = 256
N_FF_CHUNKS = D_FF // FF_CHUNK
DEEPNORM_ALPHA = (2 * DEPTH) ** 0.25
LN_EPS = 1e-5
NEG_INF = -1e30
QK_SCALE = HEAD_DIM ** -0.5

TILE = 512
SWA_UNIT_HEADS = 4
CONV_ROW_BLOCK = 64
CONV_HALO = 32
SHORT_HALO = 8
VMEM_LIMIT_BYTES = 56 * 1024 * 1024

BF16 = jnp.bfloat16
F32 = jnp.float32


def _dot(a, b):
    return jnp.dot(a, b, preferred_element_type=F32)


def _layer_norm(y, g, b):
    mu = jnp.mean(y, -1, keepdims=True)
    d = y - mu
    var = jnp.mean(d * d, -1, keepdims=True)
    return d * lax.rsqrt(var + LN_EPS) * g + b


def _t5_bucket(rel):
    nb = REL_BUCKETS // 2
    ret = (rel > 0).astype(np.int32) * nb
    n = np.abs(rel)
    max_exact = nb // 2
    large = max_exact + (np.log(np.maximum(n, 1) / max_exact)
                         / np.log(REL_MAX_DIST / max_exact) * (nb - max_exact)).astype(np.int32)
    large = np.minimum(large, nb - 1)
    return (ret + np.where(n < max_exact, n, large)).astype(np.int32)


def _kv_kernel(mem_ref, w_ref, kbd_ref, vbd_ref):
    kv = _dot(mem_ref[0].astype(BF16), w_ref[0].astype(BF16))
    kt = (kv[:, :X_WIDTH] * QK_SCALE).T
    v = kv[:, X_WIDTH:]
    row_head = lax.broadcasted_iota(jnp.int32, (X_WIDTH, MEM_LEN), 0) // HEAD_DIM
    col_head = lax.broadcasted_iota(jnp.int32, (MEM_LEN, X_WIDTH), 1) // HEAD_DIM
    for h in range(X_HEADS):
        kbd_ref[0, 0, :, h * MEM_LEN:(h + 1) * MEM_LEN] = jnp.where(row_head == h, kt, 0.0).astype(BF16)
        vbd_ref[0, 0, h * MEM_LEN:(h + 1) * MEM_LEN, :] = jnp.where(col_head == h, v, 0.0).astype(BF16)


def _memory_kv(mem, w_mem_kv):
    bsz = mem.shape[0]
    return pl.pallas_call(
        _kv_kernel,
        grid=(DEPTH, bsz),
        in_specs=[pl.BlockSpec((1, MEM_LEN, D_MODEL), lambda l, b: (b, 0, 0)),
                  pl.BlockSpec((1, D_MODEL, 2 * X_WIDTH), lambda l, b: (l, 0, 0))],
        out_specs=[pl.BlockSpec((1, 1, X_WIDTH, X_HEADS * MEM_LEN), lambda l, b: (l, b, 0, 0)),
                   pl.BlockSpec((1, 1, X_HEADS * MEM_LEN, X_WIDTH), lambda l, b: (l, b, 0, 0))],
        out_shape=[jax.ShapeDtypeStruct((DEPTH, bsz, X_WIDTH, X_HEADS * MEM_LEN), BF16),
                   jax.ShapeDtypeStruct((DEPTH, bsz, X_HEADS * MEM_LEN, X_WIDTH), BF16)],
        name="memory_kv",
    )(mem, w_mem_kv)


SINK_COLUMN = BAND
_SINK_BUCKET = -2
_PAD_BUCKET = -1


def _bias_kernel(rb_ref, sink_ref, bucket_ref, out_ref):
    h = pl.program_id(0)
    bk = bucket_ref[...]
    acc = jnp.where(bk == _SINK_BUCKET, sink_ref[h], NEG_INF)
    for b in range(REL_BUCKETS):
        acc = jnp.where(bk == b, rb_ref[b, h], acc)
    out_ref[0] = acc


def _bias_table(rel_bias, sinks):
    n_heads = rel_bias.shape[1]
    qpos = np.arange(CHUNK)[:, None]
    kpos = np.arange(BAND)[None, :] - (BAND - CHUNK)
    buckets = np.full((CHUNK, BAND_PAD), _PAD_BUCKET, np.int32)
    buckets[:, :BAND] = _t5_bucket(kpos - qpos)
    buckets[:, SINK_COLUMN] = _SINK_BUCKET
    return pl.pallas_call(
        _bias_kernel,
        grid=(n_heads,),
        in_specs=[pl.BlockSpec(memory_space=pltpu.SMEM),
                  pl.BlockSpec(memory_space=pltpu.SMEM),
                  pl.BlockSpec((CHUNK, BAND_PAD), lambda h: (0, 0))],
        out_specs=pl.BlockSpec((1, CHUNK, BAND_PAD), lambda h: (h, 0, 0)),
        out_shape=jax.ShapeDtypeStruct((n_heads, CHUNK, BAND_PAD), F32),
        name="rel_bias_table",
    )(rel_bias, sinks, jnp.asarray(buckets))


def _cross_attention(qx, kbd_ref, vbd_ref):
    s = _dot(qx.astype(BF16), kbd_ref[0])
    probs = []
    for h in range(X_HEADS):
        sh = s[:, h * MEM_LEN:(h + 1) * MEM_LEN]
        e = jnp.exp(sh - jnp.max(sh, -1, keepdims=True))
        probs.append((e / jnp.sum(e, -1, keepdims=True)).astype(BF16))
    return _dot(jnp.concatenate(probs, -1), vbd_ref[0])


def _project_and_norm(x, br_ref, wo_ref, g_ref, b_ref, out_ref):
    y = DEEPNORM_ALPHA * x + _dot(br_ref[...], wo_ref[...])
    out_ref[0] = _layer_norm(y, g_ref[...], b_ref[...])


def _sg_kernel(x_ref, win_ref, vg_ref, vb_ref, ws_ref, bs_ref, kbd_ref, vbd_ref, wo_ref, g_ref, b_ref,
               out_ref, br_ref, u_ref, v_ref):
    x = x_ref[0]
    xb = x.astype(BF16)
    u_ref[...] = jax.nn.gelu(_dot(xb, win_ref[:, :MIX_WIDTH]))
    zv = jax.nn.gelu(_dot(xb, win_ref[:, MIX_WIDTH:2 * MIX_WIDTH]))
    v_ref[...] = _layer_norm(zv, vg_ref[...], vb_ref[...]).astype(BF16)
    qx = _dot(xb, win_ref[:, 2 * MIX_WIDTH:])
    br_ref[:, MIX_WIDTH:] = _cross_attention(qx, kbd_ref, vbd_ref).astype(BF16)

    row_chunk = lax.broadcasted_iota(jnp.int32, (SG_BLOCK, SG_BLOCK), 0) // CHUNK
    col_chunk = lax.broadcasted_iota(jnp.int32, (SG_BLOCK, SG_BLOCK), 1) // CHUNK
    gch = MIX_WIDTH // SG_GROUPS
    for g in range(SG_GROUPS):
        w_m = jnp.where(col_chunk <= row_chunk, ws_ref[g], 0.0).astype(BF16)
        bias = bs_ref[g]
        for n in range(TILE // SG_BLOCK):
            rows = slice(n * SG_BLOCK, (n + 1) * SG_BLOCK)
            cols = slice(g * gch, (g + 1) * gch)
            sv = _dot(w_m, v_ref[rows, cols]) + bias
            br_ref[rows, cols] = (u_ref[rows, cols] * sv).astype(BF16)
    _project_and_norm(x, br_ref, wo_ref, g_ref, b_ref, out_ref)


def _swa_kernel(x_ref, win_ref, bias_ref, kbd_ref, vbd_ref, wo_ref, g_ref, b_ref,
                out_ref, br_ref, q_ref, k_ref, v_ref):
    s_idx = pl.program_id(1)
    halo = BAND - CHUNK
    kv_w = SWA_KV_HEADS * HEAD_DIM

    @pl.when(s_idx == 0)
    def _():
        k_ref[:, 0:halo, :] = jnp.zeros((SWA_KV_HEADS, halo, kv_w), BF16)
        v_ref[:, 0:halo, :] = jnp.zeros((SWA_KV_HEADS, halo, kv_w), BF16)

    @pl.when(s_idx != 0)
    def _():
        k_ref[:, 0:halo, :] = k_ref[:, TILE:TILE + halo, :]
        v_ref[:, 0:halo, :] = v_ref[:, TILE:TILE + halo, :]

    x = x_ref[0]
    xb = x.astype(BF16)
    q_ref[...] = (_dot(xb, win_ref[:, :MIX_WIDTH]) * QK_SCALE).astype(BF16)
    kvx = _dot(xb, win_ref[:, MIX_WIDTH:])
    lane = lax.broadcasted_iota(jnp.int32, (TILE, kv_w), 1)
    low = lane < HEAD_DIM
    for name_ref, t in ((k_ref, kvx[:, :kv_w]), (v_ref, kvx[:, kv_w:2 * kv_w])):
        swapped = pltpu.roll(t, HEAD_DIM, 1)
        name_ref[0, halo:halo + TILE, :] = jnp.where(low, t, swapped).astype(BF16)
        name_ref[1, halo:halo + TILE, :] = jnp.where(low, swapped, t).astype(BF16)
    br_ref[:, MIX_WIDTH:] = _cross_attention(kvx[:, 2 * kv_w:], kbd_ref, vbd_ref).astype(BF16)

    unit_rows = SWA_UNIT_HEADS * CHUNK
    col = lax.broadcasted_iota(jnp.int32, (unit_rows, BAND_PAD), 1)
    low_q = lax.broadcasted_iota(jnp.int32, (CHUNK, 2 * HEAD_DIM), 1) < HEAD_DIM
    pad = BAND_PAD - BAND
    k_tail = jnp.zeros((pad, kv_w), BF16)
    v_ones = jnp.ones((BAND, kv_w), BF16)
    v_tail = jnp.concatenate([jnp.zeros((pad, kv_w), BF16), jnp.ones((pad, kv_w), BF16)], 1)

    for c in range(TILE // CHUNK):
        r0 = c * CHUNK
        for kh in range(SWA_KV_HEADS):
            kb = jnp.concatenate([k_ref[kh, r0:r0 + BAND, :], k_tail], 0)
            vb = jnp.concatenate(
                [jnp.concatenate([v_ref[kh, r0:r0 + BAND, :], v_ones], 1), v_tail], 0)
            for u in range(SWA_GROUP // SWA_UNIT_HEADS):
                pairs = range(u * SWA_UNIT_HEADS // 2, (u + 1) * SWA_UNIT_HEADS // 2)
                pieces = []
                for j in pairs:
                    qp = q_ref[r0:r0 + CHUNK, kh * 512 + j * 128:kh * 512 + (j + 1) * 128]
                    pieces.append(jnp.where(low_q, qp, jnp.zeros_like(qp)))
                    pieces.append(jnp.where(low_q, jnp.zeros_like(qp), qp))
                qs = jnp.concatenate(pieces, 0)
                sc = lax.dot_general(qs, kb, (((1,), (1,)), ((), ())), preferred_element_type=F32)
                sc = sc + bias_ref[kh, u * unit_rows:(u + 1) * unit_rows, :]
                if c < BAND // CHUNK - 1:
                    n_invalid = jnp.where(s_idx == 0, (BAND // CHUNK - 1 - c) * CHUNK, 0)
                    sc = jnp.where(col < n_invalid, NEG_INF, sc)
                e = jnp.exp(sc - jnp.max(sc, -1, keepdims=True)).astype(BF16)
                o = _dot(e, vb)
                o = o[:, :kv_w] / o[:, kv_w:]
                for n, j in enumerate(pairs):
                    lo = o[(2 * n) * CHUNK:(2 * n + 1) * CHUNK]
                    hi = o[(2 * n + 1) * CHUNK:(2 * n + 2) * CHUNK]
                    br_ref[r0:r0 + CHUNK, kh * 512 + j * 128:kh * 512 + (j + 1) * 128] = (
                        jnp.where(low_q, lo, hi).astype(BF16))
    _project_and_norm(x, br_ref, wo_ref, g_ref, b_ref, out_ref)


def _causal_conv(ext_ref, w_ref, halo, width, store):
    base = halo - (width - 1)
    rb = CONV_ROW_BLOCK
    for cb in range(MIX_WIDTH // 128):
        cols = slice(cb * 128, (cb + 1) * 128)
        for r0 in range(0, TILE, rb):
            acc = None
            for shift in range(8):
                taps = [k for k in range(width) if (base + k) % 8 == shift]
                if not taps:
                    continue
                n_rows = rb if shift == 0 else rb + 8
                part = None
                for k in taps:
                    start = r0 + base + k - shift
                    term = w_ref[k:k + 1, cols] * ext_ref[start:start + n_rows, cols]
                    part = term if part is None else part + term
                part = part if shift == 0 else part[shift:shift + rb]
                acc = part if acc is None else acc + part
            store(slice(r0, r0 + rb), cols, acc)


def _conf_kernel(x_ref, win_ref, cw_ref, cb_ref, lg_ref, lb_ref, kbd_ref, vbd_ref, wo_ref, g_ref, b_ref,
                 out_ref, br_ref, ext_ref, y_ref):
    s_idx = pl.program_id(1)

    @pl.when(s_idx == 0)
    def _():
        ext_ref[0:CONV_HALO, :] = jnp.zeros((CONV_HALO, MIX_WIDTH), F32)

    @pl.when(s_idx != 0)
    def _():
        ext_ref[0:CONV_HALO, :] = ext_ref[TILE:TILE + CONV_HALO, :]

    x = x_ref[0]
    xb = x.astype(BF16)
    for c in range(MIX_WIDTH // 256):
        cols = slice(c * 256, (c + 1) * 256)
        a = _dot(xb, win_ref[:, c * 256:(c + 1) * 256])
        gate = _dot(xb, win_ref[:, MIX_WIDTH + c * 256:MIX_WIDTH + (c + 1) * 256])
        ext_ref[CONV_HALO:CONV_HALO + TILE, cols] = a * jax.nn.sigmoid(gate)
    qx = _dot(xb, win_ref[:, 2 * MIX_WIDTH:])
    br_ref[:, MIX_WIDTH:] = _cross_attention(qx, kbd_ref, vbd_ref).astype(BF16)

    def store(rows, cols, acc):
        y_ref[rows, cols] = acc + cb_ref[:, cols]

    _causal_conv(ext_ref, cw_ref, CONV_HALO, CONV_WIDTH, store)
    br_ref[:, :MIX_WIDTH] = jax.nn.silu(_layer_norm(y_ref[...], lg_ref[...], lb_ref[...])).astype(BF16)
    _project_and_norm(x, br_ref, wo_ref, g_ref, b_ref, out_ref)


def _short_kernel(x_ref, win_ref, cw_ref, kbd_ref, vbd_ref, wo_ref, g_ref, b_ref,
                  out_ref, br_ref, ext_ref, bg_ref):
    s_idx = pl.program_id(1)

    @pl.when(s_idx == 0)
    def _():
        ext_ref[0:SHORT_HALO, :] = jnp.zeros((SHORT_HALO, MIX_WIDTH), F32)

    @pl.when(s_idx != 0)
    def _():
        ext_ref[0:SHORT_HALO, :] = ext_ref[TILE:TILE + SHORT_HALO, :]

    x = x_ref[0]
    xb = x.astype(BF16)
    for c in range(MIX_WIDTH // 256):
        cols = slice(c * 256, (c + 1) * 256)
        bg_ref[:, cols] = _dot(xb, win_ref[:, c * 256:(c + 1) * 256])
        cg = _dot(xb, win_ref[:, MIX_WIDTH + c * 256:MIX_WIDTH + (c + 1) * 256])
        hv = _dot(xb, win_ref[:, 2 * MIX_WIDTH + c * 256:2 * MIX_WIDTH + (c + 1) * 256])
        ext_ref[SHORT_HALO:SHORT_HALO + TILE, cols] = cg * hv
    qx = _dot(xb, win_ref[:, 3 * MIX_WIDTH:])
    br_ref[:, MIX_WIDTH:] = _cross_attention(qx, kbd_ref, vbd_ref).astype(BF16)

    def store(rows, cols, acc):
        br_ref[rows, cols] = (bg_ref[rows, cols] * acc).astype(BF16)

    _causal_conv(ext_ref, cw_ref, SHORT_HALO, SHORT_CONV_WIDTH, store)
    _project_and_norm(x, br_ref, wo_ref, g_ref, b_ref, out_ref)


def _ffn_kernel(x_ref, win_ref, wd_ref, g_ref, b_ref, out_ref, act_ref):
    x = x_ref[0]
    xb = x.astype(BF16)
    for c in range(N_FF_CHUNKS):
        h = _dot(xb, win_ref[c])
        act_ref[:, c * FF_CHUNK:(c + 1) * FF_CHUNK] = (
            jax.nn.silu(h[:, :FF_CHUNK]) * h[:, FF_CHUNK:]).astype(BF16)
    y = DEEPNORM_ALPHA * x + _dot(act_ref[...], wd_ref[...])
    out_ref[0] = _layer_norm(y, g_ref[...], b_ref[...])


def _const_spec(shape):
    zeros = (0,) * len(shape)
    return pl.BlockSpec(shape, lambda b, s: zeros)


def _per_batch_spec(shape):
    zeros = (0,) * (len(shape) - 1)
    return pl.BlockSpec((1,) + tuple(shape[1:]), lambda b, s: (b,) + zeros)


def _token_call(body, name, x, consts, per_batch, scratch):
    bsz, seq, _ = x.shape
    x_spec = pl.BlockSpec((1, TILE, D_MODEL), lambda b, s: (b, s, 0))
    return pl.pallas_call(
        body,
        grid=(bsz, seq // TILE),
        in_specs=([x_spec] + [_const_spec(c.shape) for c in consts[0]]
                  + [_per_batch_spec(p.shape) for p in per_batch]
                  + [_const_spec(c.shape) for c in consts[1]]),
        out_specs=x_spec,
        out_shape=jax.ShapeDtypeStruct(x.shape, F32),
        scratch_shapes=scratch,
        compiler_params=pltpu.CompilerParams(
            dimension_semantics=("arbitrary", "arbitrary"),
            vmem_limit_bytes=VMEM_LIMIT_BYTES),
        name=name,
    )(x, *consts[0], *per_batch, *consts[1])


def _row(v):
    return v.reshape(1, -1)


def kernel(x, mem, a_w_in, a_v_ln_g, a_v_ln_b, a_w_s, a_b_s, b_w_in, b_sinks, rel_bias, c_w_in, c_conv_w, c_conv_b, c_ln_g, c_ln_b, d_w_in, d_conv_w, w_mem_kv, w_o, ln1_g, ln1_b, ffn_w_in, ffn_w_down, ln2_g, ln2_b):
    kbd, vbd = _memory_kv(mem, w_mem_kv)
    branch = pltpu.VMEM((TILE, BRANCH_WIDTH), BF16)
    for i in range(DEPTH):
        m, j = i % 4, i // 4
        tail = [w_o[i].astype(BF16), _row(ln1_g[i]), _row(ln1_b[i])]
        kv = [kbd[i], vbd[i]]
        if m == 0:
            bias = jnp.broadcast_to(a_b_s[j][:, :, None], (SG_GROUPS, SG_BLOCK, SG_BLOCK))
            head = [a_w_in[j].astype(BF16), _row(a_v_ln_g[j]), _row(a_v_ln_b[j]), a_w_s[j], bias]
            scratch = [branch, pltpu.VMEM((TILE, MIX_WIDTH), F32), pltpu.VMEM((TILE, MIX_WIDTH), BF16)]
            x = _token_call(_sg_kernel, "mixer_spatial_gating", x, (head, tail), kv, scratch)
        elif m == 1:
            table = _bias_table(rel_bias, b_sinks[j]).reshape(SWA_KV_HEADS, SWA_GROUP * CHUNK, BAND_PAD)
            head = [b_w_in[j].astype(BF16), table]
            ext = pltpu.VMEM((SWA_KV_HEADS, TILE + BAND - CHUNK, SWA_KV_HEADS * HEAD_DIM), BF16)
            scratch = [branch, pltpu.VMEM((TILE, MIX_WIDTH), BF16), ext, ext]
            x = _token_call(_swa_kernel, "mixer_swa", x, (head, tail), kv, scratch)
        elif m == 2:
            head = [c_w_in[j].astype(BF16), c_conv_w[j], _row(c_conv_b[j]), _row(c_ln_g[j]), _row(c_ln_b[j])]
            scratch = [branch, pltpu.VMEM((TILE + CONV_HALO, MIX_WIDTH), F32), pltpu.VMEM((TILE, MIX_WIDTH), F32)]
            x = _token_call(_conf_kernel, "mixer_conformer", x, (head, tail), kv, scratch)
        else:
            head = [d_w_in[j].astype(BF16), d_conv_w[j]]
            scratch = [branch, pltpu.VMEM((TILE + SHORT_HALO, MIX_WIDTH), F32), pltpu.VMEM((TILE, MIX_WIDTH), F32)]
            x = _token_call(_short_kernel, "mixer_short_conv", x, (head, tail), kv, scratch)

        w_in = ffn_w_in[i].astype(BF16)
        w_cat = jnp.concatenate([w_in[:, :D_FF].reshape(D_MODEL, N_FF_CHUNKS, FF_CHUNK),
                                 w_in[:, D_FF:].reshape(D_MODEL, N_FF_CHUNKS, FF_CHUNK)], -1)
        w_cat = w_cat.transpose(1, 0, 2)
        head = [w_cat, ffn_w_down[i].astype(BF16), _row(ln2_g[i]), _row(ln2_b[i])]
        x = _token_call(_ffn_kernel, "ffn_swiglu", x, (head, []), [], [pltpu.VMEM((TILE, D_FF), BF16)])
    return x
```

```python
import functools

import numpy as np
import jax
import jax.numpy as jnp
from jax import lax
from jax.experimental import pallas as pl
from jax.experimental.pallas import tpu as pltpu

D_MODEL = 1024
DEPTH = 4
CHUNK = 64
HEAD_DIM = 64
MIX_WIDTH = D_MODEL
MEM_LEN = 256
X_HEADS = 4
X_WIDTH = X_HEADS * HEAD_DIM
BRANCH_WIDTH = MIX_WIDTH + X_WIDTH
SG_BLOCK = 128
SG_GROUPS = 8
SWA_KV_HEADS = 2
SWA_GROUP = 8
BAND = 192
BAND_PAD = 256
REL_BUCKETS = 32
REL_MAX_DIST = 128
CONV_WIDTH = 31
SHORT_CONV_WIDTH = 3
D_FF = 2816
FF_CHUNK = 256
N_FF_CHUNKS = D_FF // FF_CHUNK
DEEPNORM_ALPHA = (2 * DEPTH) ** 0.25
LN_EPS = 1e-5
NEG_INF = -1e30
QK_SCALE = HEAD_DIM ** -0.5

TILE = 1024
SWA_UNIT_HEADS = 4
CONV_ROW_BLOCK = 64
EPILOGUE_ROWS = 256
CONV_HALO = 32
SHORT_HALO = 8
VMEM_LIMIT_BYTES = 56 * 1024 * 1024

BF16 = jnp.bfloat16
F32 = jnp.float32


def _dot(a, b):
    return jnp.dot(a, b, preferred_element_type=F32)


def _layer_norm(y, g, b):
    mu = jnp.mean(y, -1, keepdims=True)
    d = y - mu
    var = jnp.mean(d * d, -1, keepdims=True)
    return d * lax.rsqrt(var + LN_EPS) * g + b


def _t5_bucket(rel):
    nb = REL_BUCKETS // 2
    ret = (rel > 0).astype(np.int32) * nb
    n = np.abs(rel)
    max_exact = nb // 2
    large = max_exact + (np.log(np.maximum(n, 1) / max_exact)
                         / np.log(REL_MAX_DIST / max_exact) * (nb - max_exact)).astype(np.int32)
    large = np.minimum(large, nb - 1)
    return (ret + np.where(n < max_exact, n, large)).astype(np.int32)


def _kv_kernel(mem_ref, w_ref, kbd_ref, vbd_ref):
    kv = _dot(mem_ref[0].astype(BF16), w_ref[0].astype(BF16))
    kt = (kv[:, :X_WIDTH] * QK_SCALE).T
    v = kv[:, X_WIDTH:]
    row_head = lax.broadcasted_iota(jnp.int32, (X_WIDTH, MEM_LEN), 0) // HEAD_DIM
    col_head = lax.broadcasted_iota(jnp.int32, (MEM_LEN, X_WIDTH), 1) // HEAD_DIM
    for h in range(X_HEADS):
        kbd_ref[0, 0, :, h * MEM_LEN:(h + 1) * MEM_LEN] = jnp.where(row_head == h, kt, 0.0).astype(BF16)
        vbd_ref[0, 0, h * MEM_LEN:(h + 1) * MEM_LEN, :] = jnp.where(col_head == h, v, 0.0).astype(BF16)


def _memory_kv(mem, w_mem_kv):
    bsz = mem.shape[0]
    return pl.pallas_call(
        _kv_kernel,
        grid=(DEPTH, bsz),
        in_specs=[pl.BlockSpec((1, MEM_LEN, D_MODEL), lambda l, b: (b, 0, 0)),
                  pl.BlockSpec((1, D_MODEL, 2 * X_WIDTH), lambda l, b: (l, 0, 0))],
        out_specs=[pl.BlockSpec((1, 1, X_WIDTH, X_HEADS * MEM_LEN), lambda l, b: (l, b, 0, 0)),
                   pl.BlockSpec((1, 1, X_HEADS * MEM_LEN, X_WIDTH), lambda l, b: (l, b, 0, 0))],
        out_shape=[jax.ShapeDtypeStruct((DEPTH, bsz, X_WIDTH, X_HEADS * MEM_LEN), BF16),
                   jax.ShapeDtypeStruct((DEPTH, bsz, X_HEADS * MEM_LEN, X_WIDTH), BF16)],
        name="memory_kv",
    )(mem, w_mem_kv)


SINK_COLUMN = BAND
_SINK_BUCKET = -2
_PAD_BUCKET = -1


def _bias_kernel(rb_ref, sink_ref, bucket_ref, out_ref):
    h = pl.program_id(0)
    bk = bucket_ref[...]
    acc = jnp.where(bk == _SINK_BUCKET, sink_ref[h], NEG_INF)
    for b in range(REL_BUCKETS):
        acc = jnp.where(bk == b, rb_ref[b, h], acc)
    out_ref[0] = acc


def _bias_table(rel_bias, sinks):
    n_heads = rel_bias.shape[1]
    qpos = np.arange(CHUNK)[:, None]
    kpos = np.arange(BAND)[None, :] - (BAND - CHUNK)
    buckets = np.full((CHUNK, BAND_PAD), _PAD_BUCKET, np.int32)
    buckets[:, :BAND] = _t5_bucket(kpos - qpos)
    buckets[:, SINK_COLUMN] = _SINK_BUCKET
    return pl.pallas_call(
        _bias_kernel,
        grid=(n_heads,),
        in_specs=[pl.BlockSpec(memory_space=pltpu.SMEM),
                  pl.BlockSpec(memory_space=pltpu.SMEM),
                  pl.BlockSpec((CHUNK, BAND_PAD), lambda h: (0, 0))],
        out_specs=pl.BlockSpec((1, CHUNK, BAND_PAD), lambda h: (h, 0, 0)),
        out_shape=jax.ShapeDtypeStruct((n_heads, CHUNK, BAND_PAD), F32),
        name="rel_bias_table",
    )(rel_bias, sinks, jnp.asarray(buckets))


def _cross_attention(qx, kbd_ref, vbd_ref):
    s = _dot(qx.astype(BF16), kbd_ref[0])
    probs = []
    for h in range(X_HEADS):
        sh = s[:, h * MEM_LEN:(h + 1) * MEM_LEN]
        e = jnp.exp(sh - jnp.max(sh, -1, keepdims=True))
        probs.append((e / jnp.sum(e, -1, keepdims=True)).astype(BF16))
    return _dot(jnp.concatenate(probs, -1), vbd_ref[0])


def _project_and_norm(x_ref, act_ref, w_ref, g_ref, b_ref, out_ref):
    bounds = list(range(0, TILE - EPILOGUE_ROWS + 1, EPILOGUE_ROWS)) + [TILE - EPILOGUE_ROWS // 2, TILE]
    for r0, r1 in zip(bounds[:-1], bounds[1:]):
        rows = slice(r0, r1)
        y = DEEPNORM_ALPHA * x_ref[0, rows, :] + _dot(act_ref[rows, :], w_ref[...])
        out_ref[0, rows, :] = _layer_norm(y, g_ref[...], b_ref[...])


def _sg_kernel(x_ref, win_ref, vg_ref, vb_ref, ws_ref, bs_ref, kbd_ref, vbd_ref, wo_ref, g_ref, b_ref,
               out_ref, br_ref, u_ref, v_ref):
    x = x_ref[0]
    xb = x.astype(BF16)
    u_ref[...] = jax.nn.gelu(_dot(xb, win_ref[:, :MIX_WIDTH]))
    zv = jax.nn.gelu(_dot(xb, win_ref[:, MIX_WIDTH:2 * MIX_WIDTH]))
    v_ref[...] = _layer_norm(zv, vg_ref[...], vb_ref[...]).astype(BF16)
    qx = _dot(xb, win_ref[:, 2 * MIX_WIDTH:])
    br_ref[:, MIX_WIDTH:] = _cross_attention(qx, kbd_ref, vbd_ref).astype(BF16)

    row_chunk = lax.broadcasted_iota(jnp.int32, (SG_BLOCK, SG_BLOCK), 0) // CHUNK
    col_chunk = lax.broadcasted_iota(jnp.int32, (SG_BLOCK, SG_BLOCK), 1) // CHUNK
    gch = MIX_WIDTH // SG_GROUPS
    for g in range(SG_GROUPS):
        w_m = jnp.where(col_chunk <= row_chunk, ws_ref[g], 0.0).astype(BF16)
        bias = bs_ref[g]
        for n in range(TILE // SG_BLOCK):
            rows = slice(n * SG_BLOCK, (n + 1) * SG_BLOCK)
            cols = slice(g * gch, (g + 1) * gch)
            sv = _dot(w_m, v_ref[rows, cols]) + bias
            br_ref[rows, cols] = (u_ref[rows, cols] * sv).astype(BF16)
    _project_and_norm(x_ref, br_ref, wo_ref, g_ref, b_ref, out_ref)


def _swa_kernel(x_ref, win_ref, bias_ref, kbd_ref, vbd_ref, wo_ref, g_ref, b_ref,
                out_ref, br_ref, q_ref, k_ref, v_ref):
    s_idx = pl.program_id(1)
    halo = BAND - CHUNK
    kv_w = SWA_KV_HEADS * HEAD_DIM

    @pl.when(s_idx == 0)
    def _():
        k_ref[:, 0:halo, :] = jnp.zeros((SWA_KV_HEADS, halo, kv_w), BF16)
        v_ref[:, 0:halo, :] = jnp.zeros((SWA_KV_HEADS, halo, kv_w), BF16)

    @pl.when(s_idx != 0)
    def _():
        k_ref[:, 0:halo, :] = k_ref[:, TILE:TILE + halo, :]
        v_ref[:, 0:halo, :] = v_ref[:, TILE:TILE + halo, :]

    x = x_ref[0]
    xb = x.astype(BF16)
    q_ref[...] = (_dot(xb, win_ref[:, :MIX_WIDTH]) * QK_SCALE).astype(BF16)
    kvx = _dot(xb, win_ref[:, MIX_WIDTH:])
    lane = lax.broadcasted_iota(jnp.int32, (TILE, kv_w), 1)
    low = lane < HEAD_DIM
    for name_ref, t in ((k_ref, kvx[:, :kv_w]), (v_ref, kvx[:, kv_w:2 * kv_w])):
        swapped = pltpu.roll(t, HEAD_DIM, 1)
        name_ref[0, halo:halo + TILE, :] = jnp.where(low, t, swapped).astype(BF16)
        name_ref[1, halo:halo + TILE, :] = jnp.where(low, swapped, t).astype(BF16)
    br_ref[:, MIX_WIDTH:] = _cross_attention(kvx[:, 2 * kv_w:], kbd_ref, vbd_ref).astype(BF16)

    unit_rows = SWA_UNIT_HEADS * CHUNK
    col = lax.broadcasted_iota(jnp.int32, (unit_rows, BAND_PAD), 1)
    low_q = lax.broadcasted_iota(jnp.int32, (CHUNK, 2 * HEAD_DIM), 1) < HEAD_DIM
    pad = BAND_PAD - BAND
    k_tail = jnp.zeros((pad, kv_w), BF16)
    v_ones = jnp.ones((BAND, kv_w), BF16)
    v_tail = jnp.concatenate([jnp.zeros((pad, kv_w), BF16), jnp.ones((pad, kv_w), BF16)], 1)

    for c in range(TILE // CHUNK):
        r0 = c * CHUNK
        for kh in range(SWA_KV_HEADS):
            kb = jnp.concatenate([k_ref[kh, r0:r0 + BAND, :], k_tail], 0)
            vb = jnp.concatenate(
                [jnp.concatenate([v_ref[kh, r0:r0 + BAND, :], v_ones], 1), v_tail], 0)
            for u in range(SWA_GROUP // SWA_UNIT_HEADS):
                pairs = range(u * SWA_UNIT_HEADS // 2, (u + 1) * SWA_UNIT_HEADS // 2)
                pieces = []
                for j in pairs:
                    qp = q_ref[r0:r0 + CHUNK, kh * 512 + j * 128:kh * 512 + (j + 1) * 128]
                    pieces.append(jnp.where(low_q, qp, jnp.zeros_like(qp)))
                    pieces.append(jnp.where(low_q, jnp.zeros_like(qp), qp))
                qs = jnp.concatenate(pieces, 0)
                sc = lax.dot_general(qs, kb, (((1,), (1,)), ((), ())), preferred_element_type=F32)
                sc = sc + bias_ref[kh, u * unit_rows:(u + 1) * unit_rows, :]
                if c < BAND // CHUNK - 1:
                    n_invalid = jnp.where(s_idx == 0, (BAND // CHUNK - 1 - c) * CHUNK, 0)
                    sc = jnp.where(col < n_invalid, NEG_INF, sc)
                e = jnp.exp(sc - jnp.max(sc, -1, keepdims=True)).astype(BF16)
                o = _dot(e, vb)
                o = o[:, :kv_w] / o[:, kv_w:]
                for n, j in enumerate(pairs):
                    lo = o[(2 * n) * CHUNK:(2 * n + 1) * CHUNK]
                    hi = o[(2 * n + 1) * CHUNK:(2 * n + 2) * CHUNK]
                    br_ref[r0:r0 + CHUNK, kh * 512 + j * 128:kh * 512 + (j + 1) * 128] = (
                        jnp.where(low_q, lo, hi).astype(BF16))
    _project_and_norm(x_ref, br_ref, wo_ref, g_ref, b_ref, out_ref)


def _causal_conv(ext_ref, w_ref, halo, width, store):
    base = halo - (width - 1)
    rb = CONV_ROW_BLOCK
    for cb in range(MIX_WIDTH // 128):
        cols = slice(cb * 128, (cb + 1) * 128)
        for r0 in range(0, TILE, rb):
            acc = None
            for shift in range(8):
                taps = [k for k in range(width) if (base + k) % 8 == shift]
                if not taps:
                    continue
                n_rows = rb if shift == 0 else rb + 8
                part = None
                for k in taps:
                    start = r0 + base + k - shift
                    term = w_ref[k:k + 1, cols] * ext_ref[start:start + n_rows, cols]
                    part = term if part is None else part + term
                part = part if shift == 0 else part[shift:shift + rb]
                acc = part if acc is None else acc + part
            store(slice(r0, r0 + rb), cols, acc)


def _conf_kernel(x_ref, win_ref, cw_ref, cb_ref, lg_ref, lb_ref, kbd_ref, vbd_ref, wo_ref, g_ref, b_ref,
                 out_ref, br_ref, ext_ref, y_ref):
    s_idx = pl.program_id(1)

    @pl.when(s_idx == 0)
    def _():
        ext_ref[0:CONV_HALO, :] = jnp.zeros((CONV_HALO, MIX_WIDTH), F32)

    @pl.when(s_idx != 0)
    def _():
        ext_ref[0:CONV_HALO, :] = ext_ref[TILE:TILE + CONV_HALO, :]

    x = x_ref[0]
    xb = x.astype(BF16)
    for c in range(MIX_WIDTH // 256):
        cols = slice(c * 256, (c + 1) * 256)
        a = _dot(xb, win_ref[:, c * 256:(c + 1) * 256])
        gate = _dot(xb, win_ref[:, MIX_WIDTH + c * 256:MIX_WIDTH + (c + 1) * 256])
        ext_ref[CONV_HALO:CONV_HALO + TILE, cols] = a * jax.nn.sigmoid(gate)
    qx = _dot(xb, win_ref[:, 2 * MIX_WIDTH:])
    br_ref[:, MIX_WIDTH:] = _cross_attention(qx, kbd_ref, vbd_ref).astype(BF16)

    def store(rows, cols, acc):
        y_ref[rows, cols] = acc + cb_ref[:, cols]

    _causal_conv(ext_ref, cw_ref, CONV_HALO, CONV_WIDTH, store)
    br_ref[:, :MIX_WIDTH] = jax.nn.silu(_layer_norm(y_ref[...], lg_ref[...], lb_ref[...])).astype(BF16)
    _project_and_norm(x_ref, br_ref, wo_ref, g_ref, b_ref, out_ref)


def _short_kernel(x_ref, win_ref, cw_ref, kbd_ref, vbd_ref, wo_ref, g_ref, b_ref,
                  out_ref, br_ref, ext_ref, bg_ref):
    s_idx = pl.program_id(1)

    @pl.when(s_idx == 0)
    def _():
        ext_ref[0:SHORT_HALO, :] = jnp.zeros((SHORT_HALO, MIX_WIDTH), F32)

    @pl.when(s_idx != 0)
    def _():
        ext_ref[0:SHORT_HALO, :] = ext_ref[TILE:TILE + SHORT_HALO, :]

    x = x_ref[0]
    xb = x.astype(BF16)
    for c in range(MIX_WIDTH // 256):
        cols = slice(c * 256, (c + 1) * 256)
        bg_ref[:, cols] = _dot(xb, win_ref[:, c * 256:(c + 1) * 256])
        cg = _dot(xb, win_ref[:, MIX_WIDTH + c * 256:MIX_WIDTH + (c + 1) * 256])
        hv = _dot(xb, win_ref[:, 2 * MIX_WIDTH + c * 256:2 * MIX_WIDTH + (c + 1) * 256])
        ext_ref[SHORT_HALO:SHORT_HALO + TILE, cols] = cg * hv
    qx = _dot(xb, win_ref[:, 3 * MIX_WIDTH:])
    br_ref[:, MIX_WIDTH:] = _cross_attention(qx, kbd_ref, vbd_ref).astype(BF16)

    def store(rows, cols, acc):
        br_ref[rows, cols] = (bg_ref[rows, cols] * acc).astype(BF16)

    _causal_conv(ext_ref, cw_ref, SHORT_HALO, SHORT_CONV_WIDTH, store)
    _project_and_norm(x_ref, br_ref, wo_ref, g_ref, b_ref, out_ref)


def _ffn_kernel(x_ref, win_ref, wd_ref, g_ref, b_ref, out_ref, act_ref):
    xb = x_ref[0].astype(BF16)
    for c in range(N_FF_CHUNKS):
        gate = _dot(xb, win_ref[:, c * FF_CHUNK:(c + 1) * FF_CHUNK])
        value = _dot(xb, win_ref[:, D_FF + c * FF_CHUNK:D_FF + (c + 1) * FF_CHUNK])
        act_ref[:, c * FF_CHUNK:(c + 1) * FF_CHUNK] = (jax.nn.silu(gate) * value).astype(BF16)
    _project_and_norm(x_ref, act_ref, wd_ref, g_ref, b_ref, out_ref)


def _const_spec(shape):
    zeros = (0,) * len(shape)
    return pl.BlockSpec(shape, lambda b, s: zeros, pipeline_mode=pl.Buffered(1))


def _per_batch_spec(shape):
    zeros = (0,) * (len(shape) - 1)
    return pl.BlockSpec((1,) + tuple(shape[1:]), lambda b, s: (b,) + zeros)


def _token_call(body, name, x, consts, per_batch, scratch):
    bsz, seq, _ = x.shape
    x_spec = pl.BlockSpec((1, TILE, D_MODEL), lambda b, s: (b, s, 0))
    return pl.pallas_call(
        body,
        grid=(bsz, seq // TILE),
        in_specs=([x_spec] + [_const_spec(c.shape) for c in consts[0]]
                  + [_per_batch_spec(p.shape) for p in per_batch]
                  + [_const_spec(c.shape) for c in consts[1]]),
        out_specs=x_spec,
        out_shape=jax.ShapeDtypeStruct(x.shape, F32),
        scratch_shapes=scratch,
        compiler_params=pltpu.CompilerParams(
            dimension_semantics=("arbitrary", "arbitrary"),
            vmem_limit_bytes=VMEM_LIMIT_BYTES),
        name=name,
    )(x, *consts[0], *per_batch, *consts[1])


def _row(v):
    return v.reshape(1, -1)


def kernel(x, mem, a_w_in, a_v_ln_g, a_v_ln_b, a_w_s, a_b_s, b_w_in, b_sinks, rel_bias, c_w_in, c_conv_w, c_conv_b, c_ln_g, c_ln_b, d_w_in, d_conv_w, w_mem_kv, w_o, ln1_g, ln1_b, ffn_w_in, ffn_w_down, ln2_g, ln2_b):
    kbd, vbd = _memory_kv(mem, w_mem_kv)
    branch = pltpu.VMEM((TILE, BRANCH_WIDTH), BF16)
    for i in range(DEPTH):
        m, j = i % 4, i // 4
        tail = [w_o[i].astype(BF16), _row(ln1_g[i]), _row(ln1_b[i])]
        kv = [kbd[i], vbd[i]]
        if m == 0:
            bias = jnp.broadcast_to(a_b_s[j][:, :, None], (SG_GROUPS, SG_BLOCK, SG_BLOCK))
            head = [a_w_in[j].astype(BF16), _row(a_v_ln_g[j]), _row(a_v_ln_b[j]), a_w_s[j], bias]
            scratch = [branch, pltpu.VMEM((TILE, MIX_WIDTH), F32), pltpu.VMEM((TILE, MIX_WIDTH), BF16)]
            x = _token_call(_sg_kernel, "mixer_spatial_gating", x, (head, tail), kv, scratch)
        elif m == 1:
            table = _bias_table(rel_bias, b_sinks[j]).reshape(SWA_KV_HEADS, SWA_GROUP * CHUNK, BAND_PAD)
            head = [b_w_in[j].astype(BF16), table]
            ext = pltpu.VMEM((SWA_KV_HEADS, TILE + BAND - CHUNK, SWA_KV_HEADS * HEAD_DIM), BF16)
            scratch = [branch, pltpu.VMEM((TILE, MIX_WIDTH), BF16), ext, ext]
            x = _token_call(_swa_kernel, "mixer_swa", x, (head, tail), kv, scratch)
        elif m == 2:
            head = [c_w_in[j].astype(BF16), c_conv_w[j], _row(c_conv_b[j]), _row(c_ln_g[j]), _row(c_ln_b[j])]
            scratch = [branch, pltpu.VMEM((TILE + CONV_HALO, MIX_WIDTH), F32), pltpu.VMEM((TILE, MIX_WIDTH), F32)]
            x = _token_call(_conf_kernel, "mixer_conformer", x, (head, tail), kv, scratch)
        else:
            head = [d_w_in[j].astype(BF16), d_conv_w[j]]
            scratch = [branch, pltpu.VMEM((TILE + SHORT_HALO, MIX_WIDTH), F32), pltpu.VMEM((TILE, MIX_WIDTH), F32)]
            x = _token_call(_short_kernel, "mixer_short_conv", x, (head, tail), kv, scratch)

        head = [ffn_w_in[i].astype(BF16), ffn_w_down[i].astype(BF16), _row(ln2_g[i]), _row(ln2_b[i])]
        x = _token_call(_ffn_kernel, "ffn_swiglu", x, (head, []), [], [pltpu.VMEM((TILE, D_FF), BF16)])
    return x
```

```python
import numpy as np
import jax
import jax.numpy as jnp
from jax import lax
from jax.experimental import pallas as pl
from jax.experimental.pallas import tpu as pltpu

D_MODEL = 1024
DEPTH = 4
CHUNK = 64
HEAD_DIM = 64
MIX_WIDTH = D_MODEL
MEM_LEN = 256
X_HEADS = 4
X_WIDTH = X_HEADS * HEAD_DIM
BRANCH_WIDTH = MIX_WIDTH + X_WIDTH
SG_BLOCK = 128
SG_GROUPS = 8
SWA_KV_HEADS = 2
SWA_GROUP = 8
BAND = 192
BAND_PAD = 256
REL_BUCKETS = 32
REL_MAX_DIST = 128
CONV_WIDTH = 31
SHORT_CONV_WIDTH = 3
D_FF = 2816
FF_CHUNK = 256
N_FF_CHUNKS = D_FF // FF_CHUNK
DEEPNORM_ALPHA = (2 * DEPTH) ** 0.25
LN_EPS = 1e-5
NEG_INF = -1e30
QK_SCALE = HEAD_DIM ** -0.5

TILE = 1024
SWA_UNIT_HEADS = 8
CONV_ROW_BLOCK = 64
SUB_ROWS = TILE
EPILOGUE_ROWS = 256
CONV_HALO = 32
SHORT_HALO = 8
STAGE_ROWS = 64
VMEM_LIMIT_BYTES = 56 * 1024 * 1024

BF16 = jnp.bfloat16
F32 = jnp.float32


def _dot(a, b):
    return jnp.dot(a, b, preferred_element_type=F32)


def _layer_norm(y, g, b):
    mu = jnp.mean(y, -1, keepdims=True)
    d = y - mu
    var = jnp.mean(d * d, -1, keepdims=True)
    return d * lax.rsqrt(var + LN_EPS) * g + b


def _t5_bucket(rel):
    nb = REL_BUCKETS // 2
    ret = (rel > 0).astype(np.int32) * nb
    n = np.abs(rel)
    max_exact = nb // 2
    large = max_exact + (np.log(np.maximum(n, 1) / max_exact)
                         / np.log(REL_MAX_DIST / max_exact) * (nb - max_exact)).astype(np.int32)
    large = np.minimum(large, nb - 1)
    return (ret + np.where(n < max_exact, n, large)).astype(np.int32)


def _kv_kernel(mem_ref, w_ref, kbd_ref, vbd_ref):
    kv = _dot(mem_ref[0].astype(BF16), w_ref[0].astype(BF16))
    kt = (kv[:, :X_WIDTH] * QK_SCALE).T
    v = kv[:, X_WIDTH:]
    row_head = lax.broadcasted_iota(jnp.int32, (X_WIDTH, MEM_LEN), 0) // HEAD_DIM
    col_head = lax.broadcasted_iota(jnp.int32, (MEM_LEN, X_WIDTH), 1) // HEAD_DIM
    for h in range(X_HEADS):
        kbd_ref[0, 0, :, h * MEM_LEN:(h + 1) * MEM_LEN] = jnp.where(row_head == h, kt, 0.0).astype(BF16)
        vbd_ref[0, 0, h * MEM_LEN:(h + 1) * MEM_LEN, :] = jnp.where(col_head == h, v, 0.0).astype(BF16)


def _memory_kv(mem, w_mem_kv):
    bsz = mem.shape[0]
    return pl.pallas_call(
        _kv_kernel,
        grid=(DEPTH, bsz),
        in_specs=[pl.BlockSpec((1, MEM_LEN, D_MODEL), lambda l, b: (b, 0, 0)),
                  pl.BlockSpec((1, D_MODEL, 2 * X_WIDTH), lambda l, b: (l, 0, 0))],
        out_specs=[pl.BlockSpec((1, 1, X_WIDTH, X_HEADS * MEM_LEN), lambda l, b: (l, b, 0, 0)),
                   pl.BlockSpec((1, 1, X_HEADS * MEM_LEN, X_WIDTH), lambda l, b: (l, b, 0, 0))],
        out_shape=[jax.ShapeDtypeStruct((DEPTH, bsz, X_WIDTH, X_HEADS * MEM_LEN), BF16),
                   jax.ShapeDtypeStruct((DEPTH, bsz, X_HEADS * MEM_LEN, X_WIDTH), BF16)],
        name="memory_kv",
    )(mem, w_mem_kv)


SINK_COLUMN = BAND
_SINK_BUCKET = -2
_PAD_BUCKET = -1


def _bias_kernel(rb_ref, sink_ref, bucket_ref, out_ref):
    h = pl.program_id(0)
    bk = bucket_ref[...]
    acc = jnp.where(bk == _SINK_BUCKET, sink_ref[h], NEG_INF)
    for b in range(REL_BUCKETS):
        acc = jnp.where(bk == b, rb_ref[b, h], acc)
    out_ref[0] = acc


def _bias_table(rel_bias, sinks):
    n_heads = rel_bias.shape[1]
    qpos = np.arange(CHUNK)[:, None]
    kpos = np.arange(BAND)[None, :] - (BAND - CHUNK)
    buckets = np.full((CHUNK, BAND_PAD), _PAD_BUCKET, np.int32)
    buckets[:, :BAND] = _t5_bucket(kpos - qpos)
    buckets[:, SINK_COLUMN] = _SINK_BUCKET
    return pl.pallas_call(
        _bias_kernel,
        grid=(n_heads,),
        in_specs=[pl.BlockSpec(memory_space=pltpu.SMEM),
                  pl.BlockSpec(memory_space=pltpu.SMEM),
                  pl.BlockSpec((CHUNK, BAND_PAD), lambda h: (0, 0))],
        out_specs=pl.BlockSpec((1, CHUNK, BAND_PAD), lambda h: (h, 0, 0)),
        out_shape=jax.ShapeDtypeStruct((n_heads, CHUNK, BAND_PAD), F32),
        name="rel_bias_table",
    )(rel_bias, sinks, jnp.asarray(buckets))


def _cross_attention(qx, kbd_ref, vbd_ref):
    s = _dot(qx.astype(BF16), kbd_ref[0])
    probs = []
    for h in range(X_HEADS):
        sh = s[:, h * MEM_LEN:(h + 1) * MEM_LEN]
        e = jnp.exp(sh - jnp.max(sh, -1, keepdims=True))
        probs.append((e / jnp.sum(e, -1, keepdims=True)).astype(BF16))
    return _dot(jnp.concatenate(probs, -1), vbd_ref[0])


def _project_and_norm(x_ref, act_ref, w_ref, g_ref, b_ref, out_ref, r0, r1):
    rows = slice(r0, r1)
    y = DEEPNORM_ALPHA * x_ref[0, rows, :] + _dot(act_ref[rows, :], w_ref[...])
    out_ref[0, rows, :] = _layer_norm(y, g_ref[...], b_ref[...])


def _epilogue_blocks(r0, r1, is_last):
    bounds = list(range(r0, r1, EPILOGUE_ROWS)) + [r1]
    if is_last:
        bounds.insert(-1, r1 - EPILOGUE_ROWS // 2)
    return list(zip(bounds[:-1], bounds[1:]))


def _sub_blocks():
    return [(r0, r0 + SUB_ROWS, r0 + SUB_ROWS == TILE) for r0 in range(0, TILE, SUB_ROWS)]


def _sg_kernel(x_ref, win_ref, vg_ref, vb_ref, ws_ref, bs_ref, kbd_ref, vbd_ref, wo_ref, g_ref, b_ref,
               out_ref, br_ref, u_ref, v_ref):
    row_chunk = lax.broadcasted_iota(jnp.int32, (SG_BLOCK, SG_BLOCK), 0) // CHUNK
    col_chunk = lax.broadcasted_iota(jnp.int32, (SG_BLOCK, SG_BLOCK), 1) // CHUNK
    gch = MIX_WIDTH // SG_GROUPS
    for r0, r1, is_last in _sub_blocks():
        rows = slice(r0, r1)
        xb = x_ref[0, rows, :].astype(BF16)
        u_ref[rows, :] = jax.nn.gelu(_dot(xb, win_ref[:, :MIX_WIDTH]))
        zv = jax.nn.gelu(_dot(xb, win_ref[:, MIX_WIDTH:2 * MIX_WIDTH]))
        v_ref[rows, :] = _layer_norm(zv, vg_ref[...], vb_ref[...]).astype(BF16)
        qx = _dot(xb, win_ref[:, 2 * MIX_WIDTH:])
        br_ref[rows, MIX_WIDTH:] = _cross_attention(qx, kbd_ref, vbd_ref).astype(BF16)
        for g in range(SG_GROUPS):
            w_m = jnp.where(col_chunk <= row_chunk, ws_ref[g], 0.0).astype(BF16)
            cols = slice(g * gch, (g + 1) * gch)
            for n0 in range(r0, r1, SG_BLOCK):
                blk = slice(n0, n0 + SG_BLOCK)
                sv = _dot(w_m, v_ref[blk, cols]) + bs_ref[g]
                br_ref[blk, cols] = (u_ref[blk, cols] * sv).astype(BF16)
        for e0, e1 in _epilogue_blocks(r0, r1, is_last):
            _project_and_norm(x_ref, br_ref, wo_ref, g_ref, b_ref, out_ref, e0, e1)


def _swa_kernel(x_ref, win_ref, bias_ref, kbd_ref, vbd_ref, wo_ref, g_ref, b_ref,
                out_ref, br_ref, q_ref, k_ref, v_ref):
    s_idx = pl.program_id(1)
    halo = BAND - CHUNK
    kv_w = SWA_KV_HEADS * HEAD_DIM

    @pl.when(s_idx == 0)
    def _():
        k_ref[:, 0:halo, :] = jnp.zeros((SWA_KV_HEADS, halo, kv_w), BF16)
        v_ref[:, 0:halo, :] = jnp.zeros((SWA_KV_HEADS, halo, kv_w), BF16)

    @pl.when(s_idx != 0)
    def _():
        k_ref[:, 0:halo, :] = k_ref[:, TILE:TILE + halo, :]
        v_ref[:, 0:halo, :] = v_ref[:, TILE:TILE + halo, :]

    unit_rows = SWA_UNIT_HEADS * CHUNK
    col = lax.broadcasted_iota(jnp.int32, (unit_rows, BAND_PAD), 1)
    low_q = lax.broadcasted_iota(jnp.int32, (CHUNK, 2 * HEAD_DIM), 1) < HEAD_DIM
    low = lax.broadcasted_iota(jnp.int32, (SUB_ROWS, kv_w), 1) < HEAD_DIM
    pad = BAND_PAD - BAND
    k_tail = jnp.zeros((pad, kv_w), BF16)
    v_ones = jnp.ones((BAND, kv_w), BF16)
    v_tail = jnp.concatenate([jnp.zeros((pad, kv_w), BF16), jnp.ones((pad, kv_w), BF16)], 1)

    for s0, s1, is_last in _sub_blocks():
        rows = slice(s0, s1)
        xb = x_ref[0, rows, :].astype(BF16)
        q_ref[rows, :] = (_dot(xb, win_ref[:, :MIX_WIDTH]) * QK_SCALE).astype(BF16)
        kvx = _dot(xb, win_ref[:, MIX_WIDTH:])
        for dst_ref, t in ((k_ref, kvx[:, :kv_w]), (v_ref, kvx[:, kv_w:2 * kv_w])):
            swapped = pltpu.roll(t, HEAD_DIM, 1)
            dst_ref[0, halo + s0:halo + s1, :] = jnp.where(low, t, swapped).astype(BF16)
            dst_ref[1, halo + s0:halo + s1, :] = jnp.where(low, swapped, t).astype(BF16)
        br_ref[rows, MIX_WIDTH:] = _cross_attention(kvx[:, 2 * kv_w:], kbd_ref, vbd_ref).astype(BF16)

        for c in range(s0 // CHUNK, s1 // CHUNK):
            r0 = c * CHUNK
            for kh in range(SWA_KV_HEADS):
                kb = jnp.concatenate([k_ref[kh, r0:r0 + BAND, :], k_tail], 0)
                vb = jnp.concatenate(
                    [jnp.concatenate([v_ref[kh, r0:r0 + BAND, :], v_ones], 1), v_tail], 0)
                for u in range(SWA_GROUP // SWA_UNIT_HEADS):
                    pairs = range(u * SWA_UNIT_HEADS // 2, (u + 1) * SWA_UNIT_HEADS // 2)
                    pieces = []
                    for j in pairs:
                        qp = q_ref[r0:r0 + CHUNK, kh * 512 + j * 128:kh * 512 + (j + 1) * 128]
                        pieces.append(jnp.where(low_q, qp, jnp.zeros_like(qp)))
                        pieces.append(jnp.where(low_q, jnp.zeros_like(qp), qp))
                    qs = jnp.concatenate(pieces, 0)
                    sc = lax.dot_general(qs, kb, (((1,), (1,)), ((), ())), preferred_element_type=F32)
                    sc = sc + bias_ref[kh, u * unit_rows:(u + 1) * unit_rows, :]
                    if c < BAND // CHUNK - 1:
                        n_invalid = jnp.where(s_idx == 0, (BAND // CHUNK - 1 - c) * CHUNK, 0)
                        sc = jnp.where(col < n_invalid, NEG_INF, sc)
                    e = jnp.exp(sc - jnp.max(sc, -1, keepdims=True)).astype(BF16)
                    o = _dot(e, vb)
                    o = o[:, :kv_w] / o[:, kv_w:]
                    for n, j in enumerate(pairs):
                        lo = o[(2 * n) * CHUNK:(2 * n + 1) * CHUNK]
                        hi = o[(2 * n + 1) * CHUNK:(2 * n + 2) * CHUNK]
                        br_ref[r0:r0 + CHUNK, kh * 512 + j * 128:kh * 512 + (j + 1) * 128] = (
                            jnp.where(low_q, lo, hi).astype(BF16))
        for e0, e1 in _epilogue_blocks(s0, s1, is_last):
            _project_and_norm(x_ref, br_ref, wo_ref, g_ref, b_ref, out_ref, e0, e1)


def _causal_conv(ext_ref, w_ref, halo, width, r_start, r_end, c_start, c_end, store):
    base = halo - (width - 1)
    rb = CONV_ROW_BLOCK
    for cb in range(c_start // 128, c_end // 128):
        cols = slice(cb * 128, (cb + 1) * 128)
        for r0 in range(r_start, r_end, rb):
            acc = None
            for shift in range(8):
                taps = [k for k in range(width) if (base + k) % 8 == shift]
                if not taps:
                    continue
                n_rows = rb if shift == 0 else rb + 8
                part = None
                for k in taps:
                    start = r0 + base + k - shift
                    term = w_ref[k:k + 1, cols] * ext_ref[start:start + n_rows, cols]
                    part = term if part is None else part + term
                if shift:
                    part = pltpu.roll(part, n_rows - shift, 0)[:rb]
                acc = part if acc is None else acc + part
            store(slice(r0, r0 + rb), cols, acc)


def _conf_kernel(x_ref, win_ref, cw_ref, cb_ref, lg_ref, lb_ref, kbd_ref, vbd_ref, wo_ref, g_ref, b_ref,
                 out_ref, br_ref, ext_ref, y_ref):
    s_idx = pl.program_id(1)

    @pl.when(s_idx == 0)
    def _():
        ext_ref[0:CONV_HALO, :] = jnp.zeros((CONV_HALO, MIX_WIDTH), F32)

    @pl.when(s_idx != 0)
    def _():
        ext_ref[0:CONV_HALO, :] = ext_ref[TILE:TILE + CONV_HALO, :]

    def store(rows, cols, acc):
        y_ref[rows, cols] = acc + cb_ref[:, cols]

    for r0, r1, is_last in _sub_blocks():
        rows = slice(r0, r1)
        xb = x_ref[0, rows, :].astype(BF16)
        for c in range(MIX_WIDTH // 256):
            cols = slice(c * 256, (c + 1) * 256)
            a = _dot(xb, win_ref[:, c * 256:(c + 1) * 256])
            gate = _dot(xb, win_ref[:, MIX_WIDTH + c * 256:MIX_WIDTH + (c + 1) * 256])
            ext_ref[CONV_HALO + r0:CONV_HALO + r1, cols] = a * jax.nn.sigmoid(gate)
            if c > 0:
                _causal_conv(ext_ref, cw_ref, CONV_HALO, CONV_WIDTH, r0, r1, (c - 1) * 256, c * 256, store)
        qx = _dot(xb, win_ref[:, 2 * MIX_WIDTH:])
        _causal_conv(ext_ref, cw_ref, CONV_HALO, CONV_WIDTH, r0, r1, MIX_WIDTH - 256, MIX_WIDTH, store)
        br_ref[rows, MIX_WIDTH:] = _cross_attention(qx, kbd_ref, vbd_ref).astype(BF16)
        br_ref[rows, :MIX_WIDTH] = jax.nn.silu(
            _layer_norm(y_ref[rows, :], lg_ref[...], lb_ref[...])).astype(BF16)
        for e0, e1 in _epilogue_blocks(r0, r1, is_last):
            _project_and_norm(x_ref, br_ref, wo_ref, g_ref, b_ref, out_ref, e0, e1)


def _short_kernel(x_ref, win_ref, cw_ref, kbd_ref, vbd_ref, wo_ref, g_ref, b_ref,
                  out_ref, br_ref, ext_ref, bg_ref):
    s_idx = pl.program_id(1)

    @pl.when(s_idx == 0)
    def _():
        ext_ref[0:SHORT_HALO, :] = jnp.zeros((SHORT_HALO, MIX_WIDTH), F32)

    @pl.when(s_idx != 0)
    def _():
        ext_ref[0:SHORT_HALO, :] = ext_ref[TILE:TILE + SHORT_HALO, :]

    def store(rows, cols, acc):
        br_ref[rows, cols] = (bg_ref[rows, cols] * acc).astype(BF16)

    for r0, r1, is_last in _sub_blocks():
        rows = slice(r0, r1)
        xb = x_ref[0, rows, :].astype(BF16)
        for c in range(MIX_WIDTH // 256):
            cols = slice(c * 256, (c + 1) * 256)
            bg_ref[rows, cols] = _dot(xb, win_ref[:, c * 256:(c + 1) * 256])
            cg = _dot(xb, win_ref[:, MIX_WIDTH + c * 256:MIX_WIDTH + (c + 1) * 256])
            hv = _dot(xb, win_ref[:, 2 * MIX_WIDTH + c * 256:2 * MIX_WIDTH + (c + 1) * 256])
            ext_ref[SHORT_HALO + r0:SHORT_HALO + r1, cols] = cg * hv
            if c > 0:
                _causal_conv(ext_ref, cw_ref, SHORT_HALO, SHORT_CONV_WIDTH, r0, r1, (c - 1) * 256, c * 256, store)
        qx = _dot(xb, win_ref[:, 3 * MIX_WIDTH:])
        _causal_conv(ext_ref, cw_ref, SHORT_HALO, SHORT_CONV_WIDTH, r0, r1, MIX_WIDTH - 256, MIX_WIDTH, store)
        br_ref[rows, MIX_WIDTH:] = _cross_attention(qx, kbd_ref, vbd_ref).astype(BF16)
        for e0, e1 in _epilogue_blocks(r0, r1, is_last):
            _project_and_norm(x_ref, br_ref, wo_ref, g_ref, b_ref, out_ref, e0, e1)


def _ffn_kernel(x_ref, win_ref, wd_ref, g_ref, b_ref, out_ref, act_ref):
    xb = x_ref[0].astype(BF16)
    for c in range(N_FF_CHUNKS):
        gate = _dot(xb, win_ref[:, c * FF_CHUNK:(c + 1) * FF_CHUNK])
        value = _dot(xb, win_ref[:, D_FF + c * FF_CHUNK:D_FF + (c + 1) * FF_CHUNK])
        act_ref[:, c * FF_CHUNK:(c + 1) * FF_CHUNK] = (jax.nn.silu(gate) * value).astype(BF16)
    for r0, r1, is_last in _sub_blocks():
        for e0, e1 in _epilogue_blocks(r0, r1, is_last):
            _project_and_norm(x_ref, act_ref, wd_ref, g_ref, b_ref, out_ref, e0, e1)


def _const_spec(shape):
    zeros = (0,) * len(shape)
    return pl.BlockSpec(shape, lambda b, s: zeros, pipeline_mode=pl.Buffered(1))


def _per_batch_spec(shape):
    zeros = (0,) * (len(shape) - 1)
    return pl.BlockSpec((1,) + tuple(shape[1:]), lambda b, s: (b,) + zeros)


class _Resident:
    def __init__(self, stacked, index):
        self.stacked = stacked
        self.index = index


def _load_as_bf16(hbm_ref, index, dst_ref, stage_ref, sem):
    n_chunks = dst_ref.shape[0] // STAGE_ROWS

    def copy(c):
        slot = c % 2
        return pltpu.make_async_copy(
            hbm_ref.at[index, pl.ds(c * STAGE_ROWS, STAGE_ROWS), :], stage_ref.at[slot], sem.at[slot])

    copy(0).start()
    for c in range(n_chunks):
        if c + 1 < n_chunks:
            copy(c + 1).start()
        copy(c).wait()
        dst_ref[c * STAGE_ROWS:(c + 1) * STAGE_ROWS, :] = stage_ref[c % 2].astype(BF16)


def _token_call(body, name, x, consts, per_batch, scratch):
    bsz, seq, _ = x.shape
    x_spec = pl.BlockSpec((1, TILE, D_MODEL), lambda b, s: (b, s, 0))
    operands, specs, resident = [x], [x_spec], []

    def add_const(c):
        if isinstance(c, _Resident):
            resident.append((len(operands), c.index))
            operands.append(c.stacked)
            specs.append(pl.BlockSpec(memory_space=pl.ANY))
        else:
            operands.append(c)
            specs.append(_const_spec(c.shape))

    for c in consts[0]:
        add_const(c)
    for p in per_batch:
        operands.append(p)
        specs.append(_per_batch_spec(p.shape))
    for c in consts[1]:
        add_const(c)

    n_in, n_res = len(operands), len(resident)
    shapes = [operands[pos].shape[1:] for pos, _ in resident]
    assert all(rows % STAGE_ROWS == 0 for rows, _ in shapes)
    resident_scratch = ([pltpu.VMEM(s, BF16) for s in shapes]
                        + [pltpu.VMEM((2, STAGE_ROWS, s[1]), F32) for s in shapes]
                        + [pltpu.SemaphoreType.DMA((2,))])

    def wrapped(*refs):
        in_refs, out_ref, extra = list(refs[:n_in]), refs[n_in], refs[n_in + 1:]
        copies, stages, sem = extra[:n_res], extra[n_res:2 * n_res], extra[2 * n_res]

        @pl.when(jnp.logical_and(pl.program_id(0) == 0, pl.program_id(1) == 0))
        def _():
            for (pos, index), dst, stage in zip(resident, copies, stages):
                _load_as_bf16(in_refs[pos], index, dst, stage, sem)

        for (pos, _), dst in zip(resident, copies):
            in_refs[pos] = dst
        body(*in_refs, out_ref, *extra[2 * n_res + 1:])

    return pl.pallas_call(
        wrapped,
        grid=(bsz, seq // TILE),
        in_specs=specs,
        out_specs=x_spec,
        out_shape=jax.ShapeDtypeStruct(x.shape, F32),
        scratch_shapes=resident_scratch + scratch,
        compiler_params=pltpu.CompilerParams(
            dimension_semantics=("arbitrary", "arbitrary"),
            vmem_limit_bytes=VMEM_LIMIT_BYTES),
        name=name,
    )(*operands)


def _row(v):
    return v.reshape(1, -1)


def kernel(x, mem, a_w_in, a_v_ln_g, a_v_ln_b, a_w_s, a_b_s, b_w_in, b_sinks, rel_bias, c_w_in, c_conv_w, c_conv_b, c_ln_g, c_ln_b, d_w_in, d_conv_w, w_mem_kv, w_o, ln1_g, ln1_b, ffn_w_in, ffn_w_down, ln2_g, ln2_b):
    kbd, vbd = _memory_kv(mem, w_mem_kv)
    branch = pltpu.VMEM((TILE, BRANCH_WIDTH), BF16)
    for i in range(DEPTH):
        m, j = i % 4, i // 4
        tail = [_Resident(w_o, i), _row(ln1_g[i]), _row(ln1_b[i])]
        kv = [kbd[i], vbd[i]]
        if m == 0:
            bias = jnp.broadcast_to(a_b_s[j][:, :, None], (SG_GROUPS, SG_BLOCK, SG_BLOCK))
            head = [_Resident(a_w_in, j), _row(a_v_ln_g[j]), _row(a_v_ln_b[j]), a_w_s[j], bias]
            scratch = [branch, pltpu.VMEM((TILE, MIX_WIDTH), F32), pltpu.VMEM((TILE, MIX_WIDTH), BF16)]
            x = _token_call(_sg_kernel, "mixer_spatial_gating", x, (head, tail), kv, scratch)
        elif m == 1:
            table = _bias_table(rel_bias, b_sinks[j]).reshape(SWA_KV_HEADS, SWA_GROUP * CHUNK, BAND_PAD)
            head = [_Resident(b_w_in, j), table]
            ext = pltpu.VMEM((SWA_KV_HEADS, TILE + BAND - CHUNK, SWA_KV_HEADS * HEAD_DIM), BF16)
            scratch = [branch, pltpu.VMEM((TILE, MIX_WIDTH), BF16), ext, ext]
            x = _token_call(_swa_kernel, "mixer_swa", x, (head, tail), kv, scratch)
        elif m == 2:
            head = [_Resident(c_w_in, j), c_conv_w[j], _row(c_conv_b[j]), _row(c_ln_g[j]), _row(c_ln_b[j])]
            scratch = [branch, pltpu.VMEM((TILE + CONV_HALO, MIX_WIDTH), F32), pltpu.VMEM((TILE, MIX_WIDTH), F32)]
            x = _token_call(_conf_kernel, "mixer_conformer", x, (head, tail), kv, scratch)
        else:
            head = [_Resident(d_w_in, j), d_conv_w[j]]
            scratch = [branch, pltpu.VMEM((TILE + SHORT_HALO, MIX_WIDTH), F32), pltpu.VMEM((TILE, MIX_WIDTH), F32)]
            x = _token_call(_short_kernel, "mixer_short_conv", x, (head, tail), kv, scratch)

        head = [_Resident(ffn_w_in, i), _Resident(ffn_w_down, i), _row(ln2_g[i]), _row(ln2_b[i])]
        x = _token_call(_ffn_kernel, "ffn_swiglu", x, (head, []), [], [pltpu.VMEM((TILE, D_FF), BF16)])
    return x
```

```python
import numpy as np
import jax
import jax.numpy as jnp
from jax import lax
from jax.experimental import pallas as pl
from jax.experimental.pallas import tpu as pltpu

D_MODEL = 1024
DEPTH = 4
CHUNK = 64
HEAD_DIM = 64
MIX_WIDTH = D_MODEL
MEM_LEN = 256
X_HEADS = 4
X_WIDTH = X_HEADS * HEAD_DIM
BRANCH_WIDTH = MIX_WIDTH + X_WIDTH
SG_BLOCK = 128
SG_GROUPS = 8
SWA_KV_HEADS = 2
SWA_GROUP = 8
BAND = 192
BAND_PAD = 256
REL_BUCKETS = 32
REL_MAX_DIST = 128
CONV_WIDTH = 31
SHORT_CONV_WIDTH = 3
D_FF = 2816
FF_CHUNK = 256
N_FF_CHUNKS = D_FF // FF_CHUNK
DEEPNORM_ALPHA = (2 * DEPTH) ** 0.25
LN_EPS = 1e-5
NEG_INF = -1e30
QK_SCALE = HEAD_DIM ** -0.5

TILE = 1024
SWA_UNIT_HEADS = 8
CONV_ROW_BLOCK = 64
SUB_ROWS = TILE
EPILOGUE_ROWS = 256
CONV_HALO = 32
SHORT_HALO = 8
STAGE_ELEMS = 384 * 1024
BF16_SUBLANES = 16
VMEM_LIMIT_BYTES = 56 * 1024 * 1024

BF16 = jnp.bfloat16
F32 = jnp.float32


def _dot(a, b):
    return jnp.dot(a, b, preferred_element_type=F32)


def _layer_norm(y, g, b):
    mu = jnp.mean(y, -1, keepdims=True)
    d = y - mu
    var = jnp.mean(d * d, -1, keepdims=True)
    return d * lax.rsqrt(var + LN_EPS) * g + b


def _t5_bucket(rel):
    nb = REL_BUCKETS // 2
    ret = (rel > 0).astype(np.int32) * nb
    n = np.abs(rel)
    max_exact = nb // 2
    large = max_exact + (np.log(np.maximum(n, 1) / max_exact)
                         / np.log(REL_MAX_DIST / max_exact) * (nb - max_exact)).astype(np.int32)
    large = np.minimum(large, nb - 1)
    return (ret + np.where(n < max_exact, n, large)).astype(np.int32)


def _kv_kernel(mem_ref, w_ref, kbd_ref, vbd_ref):
    kv = _dot(mem_ref[0].astype(BF16), w_ref[0].astype(BF16))
    kt = (kv[:, :X_WIDTH] * QK_SCALE).T
    v = kv[:, X_WIDTH:]
    row_head = lax.broadcasted_iota(jnp.int32, (X_WIDTH, MEM_LEN), 0) // HEAD_DIM
    col_head = lax.broadcasted_iota(jnp.int32, (MEM_LEN, X_WIDTH), 1) // HEAD_DIM
    for h in range(X_HEADS):
        kbd_ref[0, 0, :, h * MEM_LEN:(h + 1) * MEM_LEN] = jnp.where(row_head == h, kt, 0.0).astype(BF16)
        vbd_ref[0, 0, h * MEM_LEN:(h + 1) * MEM_LEN, :] = jnp.where(col_head == h, v, 0.0).astype(BF16)


def _memory_kv(mem, w_mem_kv):
    bsz = mem.shape[0]
    return pl.pallas_call(
        _kv_kernel,
        grid=(DEPTH, bsz),
        in_specs=[pl.BlockSpec((1, MEM_LEN, D_MODEL), lambda l, b: (b, 0, 0)),
                  pl.BlockSpec((1, D_MODEL, 2 * X_WIDTH), lambda l, b: (l, 0, 0))],
        out_specs=[pl.BlockSpec((1, 1, X_WIDTH, X_HEADS * MEM_LEN), lambda l, b: (l, b, 0, 0)),
                   pl.BlockSpec((1, 1, X_HEADS * MEM_LEN, X_WIDTH), lambda l, b: (l, b, 0, 0))],
        out_shape=[jax.ShapeDtypeStruct((DEPTH, bsz, X_WIDTH, X_HEADS * MEM_LEN), BF16),
                   jax.ShapeDtypeStruct((DEPTH, bsz, X_HEADS * MEM_LEN, X_WIDTH), BF16)],
        name="memory_kv",
    )(mem, w_mem_kv)


SINK_COLUMN = BAND
_SINK_BUCKET = -2
_PAD_BUCKET = -1


def _bias_kernel(rb_ref, sink_ref, bucket_ref, out_ref):
    h = pl.program_id(0)
    bk = bucket_ref[...]
    acc = jnp.where(bk == _SINK_BUCKET, sink_ref[h], NEG_INF)
    for b in range(REL_BUCKETS):
        acc = jnp.where(bk == b, rb_ref[b, h], acc)
    out_ref[0] = acc


def _bias_table(rel_bias, sinks):
    n_heads = rel_bias.shape[1]
    qpos = np.arange(CHUNK)[:, None]
    kpos = np.arange(BAND)[None, :] - (BAND - CHUNK)
    buckets = np.full((CHUNK, BAND_PAD), _PAD_BUCKET, np.int32)
    buckets[:, :BAND] = _t5_bucket(kpos - qpos)
    buckets[:, SINK_COLUMN] = _SINK_BUCKET
    return pl.pallas_call(
        _bias_kernel,
        grid=(n_heads,),
        in_specs=[pl.BlockSpec(memory_space=pltpu.SMEM),
                  pl.BlockSpec(memory_space=pltpu.SMEM),
                  pl.BlockSpec((CHUNK, BAND_PAD), lambda h: (0, 0))],
        out_specs=pl.BlockSpec((1, CHUNK, BAND_PAD), lambda h: (h, 0, 0)),
        out_shape=jax.ShapeDtypeStruct((n_heads, CHUNK, BAND_PAD), F32),
        name="rel_bias_table",
    )(rel_bias, sinks, jnp.asarray(buckets))


def _cross_attention(qx, kbd_ref, vbd_ref):
    s = _dot(qx.astype(BF16), kbd_ref[0])
    probs = []
    for h in range(X_HEADS):
        sh = s[:, h * MEM_LEN:(h + 1) * MEM_LEN]
        e = jnp.exp(sh - jnp.max(sh, -1, keepdims=True))
        probs.append((e / jnp.sum(e, -1, keepdims=True)).astype(BF16))
    return _dot(jnp.concatenate(probs, -1), vbd_ref[0])


def _project_and_norm(x_ref, act_ref, w_ref, g_ref, b_ref, out_ref, r0, r1):
    rows = slice(r0, r1)
    y = DEEPNORM_ALPHA * x_ref[0, rows, :] + _dot(act_ref[rows, :], w_ref[...])
    out_ref[0, rows, :] = _layer_norm(y, g_ref[...], b_ref[...])


def _epilogue_blocks(r0, r1, is_last):
    bounds = list(range(r0, r1, EPILOGUE_ROWS)) + [r1]
    if is_last:
        bounds.insert(-1, r1 - EPILOGUE_ROWS // 2)
    return list(zip(bounds[:-1], bounds[1:]))


def _sub_blocks():
    return [(r0, r0 + SUB_ROWS, r0 + SUB_ROWS == TILE) for r0 in range(0, TILE, SUB_ROWS)]


def _sg_kernel(x_ref, win_ref, vg_ref, vb_ref, ws_ref, bs_ref, kbd_ref, vbd_ref, wo_ref, g_ref, b_ref,
               out_ref, br_ref, u_ref, v_ref):
    row_chunk = lax.broadcasted_iota(jnp.int32, (SG_BLOCK, SG_BLOCK), 0) // CHUNK
    col_chunk = lax.broadcasted_iota(jnp.int32, (SG_BLOCK, SG_BLOCK), 1) // CHUNK
    gch = MIX_WIDTH // SG_GROUPS
    for r0, r1, is_last in _sub_blocks():
        rows = slice(r0, r1)
        xb = x_ref[0, rows, :].astype(BF16)
        u_ref[rows, :] = jax.nn.gelu(_dot(xb, win_ref[:, :MIX_WIDTH]))
        zv = jax.nn.gelu(_dot(xb, win_ref[:, MIX_WIDTH:2 * MIX_WIDTH]))
        v_ref[rows, :] = _layer_norm(zv, vg_ref[...], vb_ref[...]).astype(BF16)
        qx = _dot(xb, win_ref[:, 2 * MIX_WIDTH:])
        br_ref[rows, MIX_WIDTH:] = _cross_attention(qx, kbd_ref, vbd_ref).astype(BF16)
        for g in range(SG_GROUPS):
            w_m = jnp.where(col_chunk <= row_chunk, ws_ref[g], 0.0).astype(BF16)
            cols = slice(g * gch, (g + 1) * gch)
            for n0 in range(r0, r1, SG_BLOCK):
                blk = slice(n0, n0 + SG_BLOCK)
                sv = _dot(w_m, v_ref[blk, cols]) + bs_ref[g]
                br_ref[blk, cols] = (u_ref[blk, cols] * sv).astype(BF16)
        for e0, e1 in _epilogue_blocks(r0, r1, is_last):
            _project_and_norm(x_ref, br_ref, wo_ref, g_ref, b_ref, out_ref, e0, e1)


def _swa_kernel(x_ref, win_ref, bias_ref, kbd_ref, vbd_ref, wo_ref, g_ref, b_ref,
                out_ref, br_ref, q_ref, k_ref, v_ref):
    s_idx = pl.program_id(1)
    halo = BAND - CHUNK
    kv_w = SWA_KV_HEADS * HEAD_DIM

    @pl.when(s_idx == 0)
    def _():
        k_ref[:, 0:halo, :] = jnp.zeros((SWA_KV_HEADS, halo, kv_w), BF16)
        v_ref[:, 0:halo, :] = jnp.zeros((SWA_KV_HEADS, halo, kv_w), BF16)

    @pl.when(s_idx != 0)
    def _():
        k_ref[:, 0:halo, :] = k_ref[:, TILE:TILE + halo, :]
        v_ref[:, 0:halo, :] = v_ref[:, TILE:TILE + halo, :]

    unit_rows = SWA_UNIT_HEADS * CHUNK
    col = lax.broadcasted_iota(jnp.int32, (unit_rows, BAND_PAD), 1)
    low_q = lax.broadcasted_iota(jnp.int32, (CHUNK, 2 * HEAD_DIM), 1) < HEAD_DIM
    low = lax.broadcasted_iota(jnp.int32, (SUB_ROWS, kv_w), 1) < HEAD_DIM
    pad = BAND_PAD - BAND
    k_tail = jnp.zeros((pad, kv_w), BF16)
    v_ones = jnp.ones((BAND, kv_w), BF16)
    v_tail = jnp.concatenate([jnp.zeros((pad, kv_w), BF16), jnp.ones((pad, kv_w), BF16)], 1)

    for s0, s1, is_last in _sub_blocks():
        rows = slice(s0, s1)
        xb = x_ref[0, rows, :].astype(BF16)
        q_ref[rows, :] = (_dot(xb, win_ref[:, :MIX_WIDTH]) * QK_SCALE).astype(BF16)
        kvx = _dot(xb, win_ref[:, MIX_WIDTH:])
        for dst_ref, t in ((k_ref, kvx[:, :kv_w]), (v_ref, kvx[:, kv_w:2 * kv_w])):
            swapped = pltpu.roll(t, HEAD_DIM, 1)
            dst_ref[0, halo + s0:halo + s1, :] = jnp.where(low, t, swapped).astype(BF16)
            dst_ref[1, halo + s0:halo + s1, :] = jnp.where(low, swapped, t).astype(BF16)
        br_ref[rows, MIX_WIDTH:] = _cross_attention(kvx[:, 2 * kv_w:], kbd_ref, vbd_ref).astype(BF16)

        for c in range(s0 // CHUNK, s1 // CHUNK):
            r0 = c * CHUNK
            for kh in range(SWA_KV_HEADS):
                kb = jnp.concatenate([k_ref[kh, r0:r0 + BAND, :], k_tail], 0)
                vb = jnp.concatenate(
                    [jnp.concatenate([v_ref[kh, r0:r0 + BAND, :], v_ones], 1), v_tail], 0)
                for u in range(SWA_GROUP // SWA_UNIT_HEADS):
                    pairs = range(u * SWA_UNIT_HEADS // 2, (u + 1) * SWA_UNIT_HEADS // 2)
                    pieces = []
                    for j in pairs:
                        qp = q_ref[r0:r0 + CHUNK, kh * 512 + j * 128:kh * 512 + (j + 1) * 128]
                        pieces.append(jnp.where(low_q, qp, jnp.zeros_like(qp)))
                        pieces.append(jnp.where(low_q, jnp.zeros_like(qp), qp))
                    qs = jnp.concatenate(pieces, 0)
                    sc = lax.dot_general(qs, kb, (((1,), (1,)), ((), ())), preferred_element_type=F32)
                    sc = sc + bias_ref[kh, u * unit_rows:(u + 1) * unit_rows, :]
                    if c < BAND // CHUNK - 1:
                        n_invalid = jnp.where(s_idx == 0, (BAND // CHUNK - 1 - c) * CHUNK, 0)
                        sc = jnp.where(col < n_invalid, NEG_INF, sc)
                    e = jnp.exp(sc - jnp.max(sc, -1, keepdims=True)).astype(BF16)
                    o = _dot(e, vb)
                    o = o[:, :kv_w] / o[:, kv_w:]
                    for n, j in enumerate(pairs):
                        lo = o[(2 * n) * CHUNK:(2 * n + 1) * CHUNK]
                        hi = o[(2 * n + 1) * CHUNK:(2 * n + 2) * CHUNK]
                        br_ref[r0:r0 + CHUNK, kh * 512 + j * 128:kh * 512 + (j + 1) * 128] = (
                            jnp.where(low_q, lo, hi).astype(BF16))
        for e0, e1 in _epilogue_blocks(s0, s1, is_last):
            _project_and_norm(x_ref, br_ref, wo_ref, g_ref, b_ref, out_ref, e0, e1)


def _causal_conv(ext_ref, w_ref, halo, width, r_start, r_end, c_start, c_end, store):
    base = halo - (width - 1)
    rb = CONV_ROW_BLOCK
    for cb in range(c_start // 128, c_end // 128):
        cols = slice(cb * 128, (cb + 1) * 128)
        for r0 in range(r_start, r_end, rb):
            acc = None
            for shift in range(8):
                taps = [k for k in range(width) if (base + k) % 8 == shift]
                if not taps:
                    continue
                n_rows = rb if shift == 0 else rb + 8
                part = None
                for k in taps:
                    start = r0 + base + k - shift
                    term = w_ref[k:k + 1, cols] * ext_ref[start:start + n_rows, cols]
                    part = term if part is None else part + term
                if shift:
                    part = pltpu.roll(part, n_rows - shift, 0)[:rb]
                acc = part if acc is None else acc + part
            store(slice(r0, r0 + rb), cols, acc)


def _conf_kernel(x_ref, win_ref, cw_ref, cb_ref, lg_ref, lb_ref, kbd_ref, vbd_ref, wo_ref, g_ref, b_ref,
                 out_ref, br_ref, ext_ref, y_ref):
    s_idx = pl.program_id(1)

    @pl.when(s_idx == 0)
    def _():
        ext_ref[0:CONV_HALO, :] = jnp.zeros((CONV_HALO, MIX_WIDTH), F32)

    @pl.when(s_idx != 0)
    def _():
        ext_ref[0:CONV_HALO, :] = ext_ref[TILE:TILE + CONV_HALO, :]

    def store(rows, cols, acc):
        y_ref[rows, cols] = acc + cb_ref[:, cols]

    for r0, r1, is_last in _sub_blocks():
        rows = slice(r0, r1)
        xb = x_ref[0, rows, :].astype(BF16)
        for c in range(MIX_WIDTH // 256):
            cols = slice(c * 256, (c + 1) * 256)
            a = _dot(xb, win_ref[:, c * 256:(c + 1) * 256])
            gate = _dot(xb, win_ref[:, MIX_WIDTH + c * 256:MIX_WIDTH + (c + 1) * 256])
            ext_ref[CONV_HALO + r0:CONV_HALO + r1, cols] = a * jax.nn.sigmoid(gate)
            if c > 0:
                _causal_conv(ext_ref, cw_ref, CONV_HALO, CONV_WIDTH, r0, r1, (c - 1) * 256, c * 256, store)
        qx = _dot(xb, win_ref[:, 2 * MIX_WIDTH:])
        _causal_conv(ext_ref, cw_ref, CONV_HALO, CONV_WIDTH, r0, r1, MIX_WIDTH - 256, MIX_WIDTH, store)
        br_ref[rows, MIX_WIDTH:] = _cross_attention(qx, kbd_ref, vbd_ref).astype(BF16)
        br_ref[rows, :MIX_WIDTH] = jax.nn.silu(
            _layer_norm(y_ref[rows, :], lg_ref[...], lb_ref[...])).astype(BF16)
        for e0, e1 in _epilogue_blocks(r0, r1, is_last):
            _project_and_norm(x_ref, br_ref, wo_ref, g_ref, b_ref, out_ref, e0, e1)


def _short_kernel(x_ref, win_ref, cw_ref, kbd_ref, vbd_ref, wo_ref, g_ref, b_ref,
                  out_ref, br_ref, ext_ref, bg_ref):
    s_idx = pl.program_id(1)

    @pl.when(s_idx == 0)
    def _():
        ext_ref[0:SHORT_HALO, :] = jnp.zeros((SHORT_HALO, MIX_WIDTH), F32)

    @pl.when(s_idx != 0)
    def _():
        ext_ref[0:SHORT_HALO, :] = ext_ref[TILE:TILE + SHORT_HALO, :]

    def store(rows, cols, acc):
        br_ref[rows, cols] = (bg_ref[rows, cols] * acc).astype(BF16)

    for r0, r1, is_last in _sub_blocks():
        rows = slice(r0, r1)
        xb = x_ref[0, rows, :].astype(BF16)
        for c in range(MIX_WIDTH // 256):
            cols = slice(c * 256, (c + 1) * 256)
            bg_ref[rows, cols] = _dot(xb, win_ref[:, c * 256:(c + 1) * 256])
            cg = _dot(xb, win_ref[:, MIX_WIDTH + c * 256:MIX_WIDTH + (c + 1) * 256])
            hv = _dot(xb, win_ref[:, 2 * MIX_WIDTH + c * 256:2 * MIX_WIDTH + (c + 1) * 256])
            ext_ref[SHORT_HALO + r0:SHORT_HALO + r1, cols] = cg * hv
            if c > 0:
                _causal_conv(ext_ref, cw_ref, SHORT_HALO, SHORT_CONV_WIDTH, r0, r1, (c - 1) * 256, c * 256, store)
        qx = _dot(xb, win_ref[:, 3 * MIX_WIDTH:])
        _causal_conv(ext_ref, cw_ref, SHORT_HALO, SHORT_CONV_WIDTH, r0, r1, MIX_WIDTH - 256, MIX_WIDTH, store)
        br_ref[rows, MIX_WIDTH:] = _cross_attention(qx, kbd_ref, vbd_ref).astype(BF16)
        for e0, e1 in _epilogue_blocks(r0, r1, is_last):
            _project_and_norm(x_ref, br_ref, wo_ref, g_ref, b_ref, out_ref, e0, e1)


def _ffn_kernel(x_ref, win_ref, wd_ref, g_ref, b_ref, out_ref, act_ref):
    xb = x_ref[0].astype(BF16)
    for c in range(N_FF_CHUNKS):
        gate = _dot(xb, win_ref[:, c * FF_CHUNK:(c + 1) * FF_CHUNK])
        value = _dot(xb, win_ref[:, D_FF + c * FF_CHUNK:D_FF + (c + 1) * FF_CHUNK])
        act_ref[:, c * FF_CHUNK:(c + 1) * FF_CHUNK] = (jax.nn.silu(gate) * value).astype(BF16)
    for r0, r1, is_last in _sub_blocks():
        for e0, e1 in _epilogue_blocks(r0, r1, is_last):
            _project_and_norm(x_ref, act_ref, wd_ref, g_ref, b_ref, out_ref, e0, e1)


def _const_spec(shape):
    zeros = (0,) * len(shape)
    return pl.BlockSpec(shape, lambda b, s: zeros, pipeline_mode=pl.Buffered(1))


def _per_batch_spec(shape):
    zeros = (0,) * (len(shape) - 1)
    return pl.BlockSpec((1,) + tuple(shape[1:]), lambda b, s: (b,) + zeros)


class _Resident:
    def __init__(self, stacked, index):
        self.stacked = stacked
        self.index = index


def _load_as_bf16(hbm_ref, index, dst_ref, stage_ref, sem):
    rows = stage_ref.shape[1]
    n_chunks = dst_ref.shape[0] // rows

    def copy(c):
        slot = c % 2
        return pltpu.make_async_copy(
            hbm_ref.at[index, pl.ds(c * rows, rows), :], stage_ref.at[slot], sem.at[slot])

    copy(0).start()
    for c in range(n_chunks):
        if c + 1 < n_chunks:
            copy(c + 1).start()
        copy(c).wait()
        dst_ref[c * rows:(c + 1) * rows, :] = stage_ref[c % 2].astype(BF16)


def _stage_rows(rows, cols):
    fits = [d for d in range(BF16_SUBLANES, rows + 1, BF16_SUBLANES)
            if rows % d == 0 and d * cols <= STAGE_ELEMS]
    return max(fits)


def _token_call(body, name, x, consts, per_batch, scratch):
    bsz, seq, _ = x.shape
    x_spec = pl.BlockSpec((1, TILE, D_MODEL), lambda b, s: (b, s, 0))
    operands, specs, resident = [x], [x_spec], []

    def add_const(c):
        if isinstance(c, _Resident):
            resident.append((len(operands), c.index))
            operands.append(c.stacked)
            specs.append(pl.BlockSpec(memory_space=pl.ANY))
        else:
            operands.append(c)
            specs.append(_const_spec(c.shape))

    for c in consts[0]:
        add_const(c)
    for p in per_batch:
        operands.append(p)
        specs.append(_per_batch_spec(p.shape))
    for c in consts[1]:
        add_const(c)

    n_in, n_res = len(operands), len(resident)
    shapes = [operands[pos].shape[1:] for pos, _ in resident]
    resident_scratch = ([pltpu.VMEM(s, BF16) for s in shapes]
                        + [pltpu.VMEM((2, _stage_rows(*s), s[1]), F32) for s in shapes]
                        + [pltpu.SemaphoreType.DMA((2,))])

    def wrapped(*refs):
        in_refs, out_ref, extra = list(refs[:n_in]), refs[n_in], refs[n_in + 1:]
        copies, stages, sem = extra[:n_res], extra[n_res:2 * n_res], extra[2 * n_res]

        @pl.when(jnp.logical_and(pl.program_id(0) == 0, pl.program_id(1) == 0))
        def _():
            for (pos, index), dst, stage in zip(resident, copies, stages):
                _load_as_bf16(in_refs[pos], index, dst, stage, sem)

        for (pos, _), dst in zip(resident, copies):
            in_refs[pos] = dst
        body(*in_refs, out_ref, *extra[2 * n_res + 1:])

    return pl.pallas_call(
        wrapped,
        grid=(bsz, seq // TILE),
        in_specs=specs,
        out_specs=x_spec,
        out_shape=jax.ShapeDtypeStruct(x.shape, F32),
        scratch_shapes=resident_scratch + scratch,
        compiler_params=pltpu.CompilerParams(
            dimension_semantics=("arbitrary", "arbitrary"),
            vmem_limit_bytes=VMEM_LIMIT_BYTES),
        name=name,
    )(*operands)


def _row(v):
    return v.reshape(1, -1)


def kernel(x, mem, a_w_in, a_v_ln_g, a_v_ln_b, a_w_s, a_b_s, b_w_in, b_sinks, rel_bias, c_w_in, c_conv_w, c_conv_b, c_ln_g, c_ln_b, d_w_in, d_conv_w, w_mem_kv, w_o, ln1_g, ln1_b, ffn_w_in, ffn_w_down, ln2_g, ln2_b):
    kbd, vbd = _memory_kv(mem, w_mem_kv)
    branch = pltpu.VMEM((TILE, BRANCH_WIDTH), BF16)
    for i in range(DEPTH):
        m, j = i % 4, i // 4
        tail = [_Resident(w_o, i), _row(ln1_g[i]), _row(ln1_b[i])]
        kv = [kbd[i], vbd[i]]
        if m == 0:
            bias = jnp.broadcast_to(a_b_s[j][:, :, None], (SG_GROUPS, SG_BLOCK, SG_BLOCK))
            head = [_Resident(a_w_in, j), _row(a_v_ln_g[j]), _row(a_v_ln_b[j]), a_w_s[j], bias]
            scratch = [branch, pltpu.VMEM((TILE, MIX_WIDTH), F32), pltpu.VMEM((TILE, MIX_WIDTH), BF16)]
            x = _token_call(_sg_kernel, "mixer_spatial_gating", x, (head, tail), kv, scratch)
        elif m == 1:
            table = _bias_table(rel_bias, b_sinks[j]).reshape(SWA_KV_HEADS, SWA_GROUP * CHUNK, BAND_PAD)
            head = [_Resident(b_w_in, j), table]
            ext = pltpu.VMEM((SWA_KV_HEADS, TILE + BAND - CHUNK, SWA_KV_HEADS * HEAD_DIM), BF16)
            scratch = [branch, pltpu.VMEM((TILE, MIX_WIDTH), BF16), ext, ext]
            x = _token_call(_swa_kernel, "mixer_swa", x, (head, tail), kv, scratch)
        elif m == 2:
            head = [_Resident(c_w_in, j), c_conv_w[j], _row(c_conv_b[j]), _row(c_ln_g[j]), _row(c_ln_b[j])]
            scratch = [branch, pltpu.VMEM((TILE + CONV_HALO, MIX_WIDTH), F32), pltpu.VMEM((TILE, MIX_WIDTH), F32)]
            x = _token_call(_conf_kernel, "mixer_conformer", x, (head, tail), kv, scratch)
        else:
            head = [_Resident(d_w_in, j), d_conv_w[j]]
            scratch = [branch, pltpu.VMEM((TILE + SHORT_HALO, MIX_WIDTH), F32), pltpu.VMEM((TILE, MIX_WIDTH), F32)]
            x = _token_call(_short_kernel, "mixer_short_conv", x, (head, tail), kv, scratch)

        head = [_Resident(ffn_w_in, i), _Resident(ffn_w_down, i), _row(ln2_g[i]), _row(ln2_b[i])]
        x = _token_call(_ffn_kernel, "ffn_swiglu", x, (head, []), [], [pltpu.VMEM((TILE, D_FF), BF16)])
    return x
```

```python
import numpy as np
import jax
import jax.numpy as jnp
from jax import lax
from jax.experimental import pallas as pl
from jax.experimental.pallas import tpu as pltpu

D_MODEL = 1024
DEPTH = 4
CHUNK = 64
HEAD_DIM = 64
MIX_WIDTH = D_MODEL
MEM_LEN = 256
X_HEADS = 4
X_WIDTH = X_HEADS * HEAD_DIM
BRANCH_WIDTH = MIX_WIDTH + X_WIDTH
SG_BLOCK = 128
SG_GROUPS = 8
SWA_KV_HEADS = 2
SWA_GROUP = 8
BAND = 192
BAND_PAD = 256
REL_BUCKETS = 32
REL_MAX_DIST = 128
CONV_WIDTH = 31
SHORT_CONV_WIDTH = 3
D_FF = 2816
FF_CHUNK = 256
N_FF_CHUNKS = D_FF // FF_CHUNK
DEEPNORM_ALPHA = (2 * DEPTH) ** 0.25
LN_EPS = 1e-5
NEG_INF = -1e30
QK_SCALE = HEAD_DIM ** -0.5

TILE = 1024
SWA_UNIT_HEADS = 8
CONV_ROW_BLOCK = 128
SUB_ROWS = TILE
EPILOGUE_ROWS = 256
CONV_HALO = 32
SHORT_HALO = 8
STAGE_ELEMS = 256 * 1024
STAGE_SLOTS = 3
BF16_SUBLANES = 16
VMEM_LIMIT_BYTES = 56 * 1024 * 1024

BF16 = jnp.bfloat16
F32 = jnp.float32


def _dot(a, b):
    return jnp.dot(a, b, preferred_element_type=F32)


def _layer_norm(y, g, b):
    mu = jnp.mean(y, -1, keepdims=True)
    d = y - mu
    var = jnp.mean(d * d, -1, keepdims=True)
    return d * lax.rsqrt(var + LN_EPS) * g + b


def _t5_bucket(rel):
    nb = REL_BUCKETS // 2
    ret = (rel > 0).astype(np.int32) * nb
    n = np.abs(rel)
    max_exact = nb // 2
    large = max_exact + (np.log(np.maximum(n, 1) / max_exact)
                         / np.log(REL_MAX_DIST / max_exact) * (nb - max_exact)).astype(np.int32)
    large = np.minimum(large, nb - 1)
    return (ret + np.where(n < max_exact, n, large)).astype(np.int32)


def _kv_kernel(mem_ref, w_ref, kbd_ref, vbd_ref):
    kv = _dot(mem_ref[0].astype(BF16), w_ref[0].astype(BF16))
    kt = (kv[:, :X_WIDTH] * QK_SCALE).T
    v = kv[:, X_WIDTH:]
    row_head = lax.broadcasted_iota(jnp.int32, (X_WIDTH, MEM_LEN), 0) // HEAD_DIM
    col_head = lax.broadcasted_iota(jnp.int32, (MEM_LEN, X_WIDTH), 1) // HEAD_DIM
    for h in range(X_HEADS):
        kbd_ref[0, 0, :, h * MEM_LEN:(h + 1) * MEM_LEN] = jnp.where(row_head == h, kt, 0.0).astype(BF16)
        vbd_ref[0, 0, h * MEM_LEN:(h + 1) * MEM_LEN, :] = jnp.where(col_head == h, v, 0.0).astype(BF16)


def _memory_kv(mem, w_mem_kv):
    bsz = mem.shape[0]
    return pl.pallas_call(
        _kv_kernel,
        grid=(DEPTH, bsz),
        in_specs=[pl.BlockSpec((1, MEM_LEN, D_MODEL), lambda l, b: (b, 0, 0)),
                  pl.BlockSpec((1, D_MODEL, 2 * X_WIDTH), lambda l, b: (l, 0, 0))],
        out_specs=[pl.BlockSpec((1, 1, X_WIDTH, X_HEADS * MEM_LEN), lambda l, b: (l, b, 0, 0)),
                   pl.BlockSpec((1, 1, X_HEADS * MEM_LEN, X_WIDTH), lambda l, b: (l, b, 0, 0))],
        out_shape=[jax.ShapeDtypeStruct((DEPTH, bsz, X_WIDTH, X_HEADS * MEM_LEN), BF16),
                   jax.ShapeDtypeStruct((DEPTH, bsz, X_HEADS * MEM_LEN, X_WIDTH), BF16)],
        name="memory_kv",
    )(mem, w_mem_kv)


SINK_COLUMN = BAND
_SINK_BUCKET = -2
_PAD_BUCKET = -1


def _bias_kernel(rb_ref, sink_ref, bucket_ref, out_ref):
    h = pl.program_id(0)
    bk = bucket_ref[...]
    acc = jnp.where(bk == _SINK_BUCKET, sink_ref[h], NEG_INF)
    for b in range(REL_BUCKETS):
        acc = jnp.where(bk == b, rb_ref[b, h], acc)
    out_ref[0] = acc


def _bias_table(rel_bias, sinks):
    n_heads = rel_bias.shape[1]
    qpos = np.arange(CHUNK)[:, None]
    kpos = np.arange(BAND)[None, :] - (BAND - CHUNK)
    buckets = np.full((CHUNK, BAND_PAD), _PAD_BUCKET, np.int32)
    buckets[:, :BAND] = _t5_bucket(kpos - qpos)
    buckets[:, SINK_COLUMN] = _SINK_BUCKET
    return pl.pallas_call(
        _bias_kernel,
        grid=(n_heads,),
        in_specs=[pl.BlockSpec(memory_space=pltpu.SMEM),
                  pl.BlockSpec(memory_space=pltpu.SMEM),
                  pl.BlockSpec((CHUNK, BAND_PAD), lambda h: (0, 0))],
        out_specs=pl.BlockSpec((1, CHUNK, BAND_PAD), lambda h: (h, 0, 0)),
        out_shape=jax.ShapeDtypeStruct((n_heads, CHUNK, BAND_PAD), F32),
        name="rel_bias_table",
    )(rel_bias, sinks, jnp.asarray(buckets))


def _cross_attention(qx, kbd_ref, vbd_ref):
    s = _dot(qx.astype(BF16), kbd_ref[0])
    probs = []
    for h in range(X_HEADS):
        sh = s[:, h * MEM_LEN:(h + 1) * MEM_LEN]
        e = jnp.exp(sh - jnp.max(sh, -1, keepdims=True))
        probs.append((e / jnp.sum(e, -1, keepdims=True)).astype(BF16))
    return _dot(jnp.concatenate(probs, -1), vbd_ref[0])


def _project_and_norm(x_ref, act_ref, w_ref, g_ref, b_ref, out_ref, r0, r1):
    rows = slice(r0, r1)
    y = DEEPNORM_ALPHA * x_ref[0, rows, :] + _dot(act_ref[rows, :], w_ref[...])
    out_ref[0, rows, :] = _layer_norm(y, g_ref[...], b_ref[...])


def _epilogue_blocks(r0, r1, is_last):
    bounds = list(range(r0, r1, EPILOGUE_ROWS)) + [r1]
    if is_last:
        bounds.insert(-1, r1 - EPILOGUE_ROWS // 2)
    return list(zip(bounds[:-1], bounds[1:]))


def _sub_blocks():
    return [(r0, r0 + SUB_ROWS, r0 + SUB_ROWS == TILE) for r0 in range(0, TILE, SUB_ROWS)]


def _sg_kernel(x_ref, win_ref, vg_ref, vb_ref, ws_ref, bs_ref, kbd_ref, vbd_ref, wo_ref, g_ref, b_ref,
               out_ref, br_ref, u_ref, v_ref):
    row_chunk = lax.broadcasted_iota(jnp.int32, (SG_BLOCK, SG_BLOCK), 0) // CHUNK
    col_chunk = lax.broadcasted_iota(jnp.int32, (SG_BLOCK, SG_BLOCK), 1) // CHUNK
    gch = MIX_WIDTH // SG_GROUPS
    for r0, r1, is_last in _sub_blocks():
        rows = slice(r0, r1)
        xb = x_ref[0, rows, :].astype(BF16)
        u_ref[rows, :] = jax.nn.gelu(_dot(xb, win_ref[:, :MIX_WIDTH]))
        zv = jax.nn.gelu(_dot(xb, win_ref[:, MIX_WIDTH:2 * MIX_WIDTH]))
        v_ref[rows, :] = _layer_norm(zv, vg_ref[...], vb_ref[...]).astype(BF16)
        qx = _dot(xb, win_ref[:, 2 * MIX_WIDTH:])
        br_ref[rows, MIX_WIDTH:] = _cross_attention(qx, kbd_ref, vbd_ref).astype(BF16)
        for g in range(SG_GROUPS):
            w_m = jnp.where(col_chunk <= row_chunk, ws_ref[g], 0.0).astype(BF16)
            cols = slice(g * gch, (g + 1) * gch)
            for n0 in range(r0, r1, SG_BLOCK):
                blk = slice(n0, n0 + SG_BLOCK)
                sv = _dot(w_m, v_ref[blk, cols]) + bs_ref[g]
                br_ref[blk, cols] = (u_ref[blk, cols] * sv).astype(BF16)
        for e0, e1 in _epilogue_blocks(r0, r1, is_last):
            _project_and_norm(x_ref, br_ref, wo_ref, g_ref, b_ref, out_ref, e0, e1)


def _swa_kernel(x_ref, win_ref, bias_ref, kbd_ref, vbd_ref, wo_ref, g_ref, b_ref,
                out_ref, br_ref, q_ref, k_ref, v_ref):
    s_idx = pl.program_id(1)
    halo = BAND - CHUNK
    kv_w = SWA_KV_HEADS * HEAD_DIM

    @pl.when(s_idx == 0)
    def _():
        k_ref[:, 0:halo, :] = jnp.zeros((SWA_KV_HEADS, halo, kv_w), BF16)
        v_ref[:, 0:halo, :] = jnp.zeros((SWA_KV_HEADS, halo, kv_w), BF16)

    @pl.when(s_idx != 0)
    def _():
        k_ref[:, 0:halo, :] = k_ref[:, TILE:TILE + halo, :]
        v_ref[:, 0:halo, :] = v_ref[:, TILE:TILE + halo, :]

    unit_rows = SWA_UNIT_HEADS * CHUNK
    col = lax.broadcasted_iota(jnp.int32, (unit_rows, BAND_PAD), 1)
    low_q = lax.broadcasted_iota(jnp.int32, (CHUNK, 2 * HEAD_DIM), 1) < HEAD_DIM
    low = lax.broadcasted_iota(jnp.int32, (SUB_ROWS, kv_w), 1) < HEAD_DIM
    pad = BAND_PAD - BAND
    k_tail = jnp.zeros((pad, kv_w), BF16)
    v_ones = jnp.ones((BAND, kv_w), BF16)
    v_tail = jnp.concatenate([jnp.zeros((pad, kv_w), BF16), jnp.ones((pad, kv_w), BF16)], 1)

    for s0, s1, is_last in _sub_blocks():
        rows = slice(s0, s1)
        xb = x_ref[0, rows, :].astype(BF16)
        q_ref[rows, :] = (_dot(xb, win_ref[:, :MIX_WIDTH]) * QK_SCALE).astype(BF16)
        kvx = _dot(xb, win_ref[:, MIX_WIDTH:])
        for dst_ref, t in ((k_ref, kvx[:, :kv_w]), (v_ref, kvx[:, kv_w:2 * kv_w])):
            swapped = pltpu.roll(t, HEAD_DIM, 1)
            dst_ref[0, halo + s0:halo + s1, :] = jnp.where(low, t, swapped).astype(BF16)
            dst_ref[1, halo + s0:halo + s1, :] = jnp.where(low, swapped, t).astype(BF16)
        br_ref[rows, MIX_WIDTH:] = _cross_attention(kvx[:, 2 * kv_w:], kbd_ref, vbd_ref).astype(BF16)

        for c in range(s0 // CHUNK, s1 // CHUNK):
            r0 = c * CHUNK
            for kh in range(SWA_KV_HEADS):
                kb = jnp.concatenate([k_ref[kh, r0:r0 + BAND, :], k_tail], 0)
                vb = jnp.concatenate(
                    [jnp.concatenate([v_ref[kh, r0:r0 + BAND, :], v_ones], 1), v_tail], 0)
                for u in range(SWA_GROUP // SWA_UNIT_HEADS):
                    pairs = range(u * SWA_UNIT_HEADS // 2, (u + 1) * SWA_UNIT_HEADS // 2)
                    pieces = []
                    for j in pairs:
                        qp = q_ref[r0:r0 + CHUNK, kh * 512 + j * 128:kh * 512 + (j + 1) * 128]
                        pieces.append(jnp.where(low_q, qp, jnp.zeros_like(qp)))
                        pieces.append(jnp.where(low_q, jnp.zeros_like(qp), qp))
                    qs = jnp.concatenate(pieces, 0)
                    sc = lax.dot_general(qs, kb, (((1,), (1,)), ((), ())), preferred_element_type=F32)
                    sc = sc + bias_ref[kh, u * unit_rows:(u + 1) * unit_rows, :]
                    if c < BAND // CHUNK - 1:
                        n_invalid = jnp.where(s_idx == 0, (BAND // CHUNK - 1 - c) * CHUNK, 0)
                        sc = jnp.where(col < n_invalid, NEG_INF, sc)
                    e = jnp.exp(sc - jnp.max(sc, -1, keepdims=True)).astype(BF16)
                    o = _dot(e, vb)
                    o = o[:, :kv_w] / o[:, kv_w:]
                    for n, j in enumerate(pairs):
                        lo = o[(2 * n) * CHUNK:(2 * n + 1) * CHUNK]
                        hi = o[(2 * n + 1) * CHUNK:(2 * n + 2) * CHUNK]
                        br_ref[r0:r0 + CHUNK, kh * 512 + j * 128:kh * 512 + (j + 1) * 128] = (
                            jnp.where(low_q, lo, hi).astype(BF16))
        for e0, e1 in _epilogue_blocks(s0, s1, is_last):
            _project_and_norm(x_ref, br_ref, wo_ref, g_ref, b_ref, out_ref, e0, e1)


def _causal_conv(ext_ref, w_ref, halo, width, r_start, r_end, c_start, c_end, store):
    base = halo - (width - 1)
    rb = CONV_ROW_BLOCK
    for cb in range(c_start // 128, c_end // 128):
        cols = slice(cb * 128, (cb + 1) * 128)
        for r0 in range(r_start, r_end, rb):
            acc = None
            for shift in range(8):
                taps = [k for k in range(width) if (base + k) % 8 == shift]
                if not taps:
                    continue
                n_rows = rb if shift == 0 else rb + 8
                part = None
                for k in taps:
                    start = r0 + base + k - shift
                    term = w_ref[k:k + 1, cols] * ext_ref[start:start + n_rows, cols]
                    part = term if part is None else part + term
                if shift:
                    part = pltpu.roll(part, n_rows - shift, 0)[:rb]
                acc = part if acc is None else acc + part
            store(slice(r0, r0 + rb), cols, acc)


def _conf_kernel(x_ref, win_ref, cw_ref, cb_ref, lg_ref, lb_ref, kbd_ref, vbd_ref, wo_ref, g_ref, b_ref,
                 out_ref, br_ref, ext_ref, y_ref):
    s_idx = pl.program_id(1)

    @pl.when(s_idx == 0)
    def _():
        ext_ref[0:CONV_HALO, :] = jnp.zeros((CONV_HALO, MIX_WIDTH), F32)

    @pl.when(s_idx != 0)
    def _():
        ext_ref[0:CONV_HALO, :] = ext_ref[TILE:TILE + CONV_HALO, :]

    def store(rows, cols, acc):
        y_ref[rows, cols] = acc + cb_ref[:, cols]

    for r0, r1, is_last in _sub_blocks():
        rows = slice(r0, r1)
        xb = x_ref[0, rows, :].astype(BF16)
        for c in range(MIX_WIDTH // 256):
            cols = slice(c * 256, (c + 1) * 256)
            a = _dot(xb, win_ref[:, c * 256:(c + 1) * 256])
            gate = _dot(xb, win_ref[:, MIX_WIDTH + c * 256:MIX_WIDTH + (c + 1) * 256])
            ext_ref[CONV_HALO + r0:CONV_HALO + r1, cols] = a * jax.nn.sigmoid(gate)
            if c > 0:
                _causal_conv(ext_ref, cw_ref, CONV_HALO, CONV_WIDTH, r0, r1, (c - 1) * 256, c * 256, store)
        qx = _dot(xb, win_ref[:, 2 * MIX_WIDTH:])
        _causal_conv(ext_ref, cw_ref, CONV_HALO, CONV_WIDTH, r0, r1, MIX_WIDTH - 256, MIX_WIDTH, store)
        br_ref[rows, MIX_WIDTH:] = _cross_attention(qx, kbd_ref, vbd_ref).astype(BF16)
        br_ref[rows, :MIX_WIDTH] = jax.nn.silu(
            _layer_norm(y_ref[rows, :], lg_ref[...], lb_ref[...])).astype(BF16)
        for e0, e1 in _epilogue_blocks(r0, r1, is_last):
            _project_and_norm(x_ref, br_ref, wo_ref, g_ref, b_ref, out_ref, e0, e1)


def _short_kernel(x_ref, win_ref, cw_ref, kbd_ref, vbd_ref, wo_ref, g_ref, b_ref,
                  out_ref, br_ref, ext_ref, bg_ref):
    s_idx = pl.program_id(1)

    @pl.when(s_idx == 0)
    def _():
        ext_ref[0:SHORT_HALO, :] = jnp.zeros((SHORT_HALO, MIX_WIDTH), F32)

    @pl.when(s_idx != 0)
    def _():
        ext_ref[0:SHORT_HALO, :] = ext_ref[TILE:TILE + SHORT_HALO, :]

    def store(rows, cols, acc):
        br_ref[rows, cols] = (bg_ref[rows, cols] * acc).astype(BF16)

    for r0, r1, is_last in _sub_blocks():
        rows = slice(r0, r1)
        xb = x_ref[0, rows, :].astype(BF16)
        for c in range(MIX_WIDTH // 256):
            cols = slice(c * 256, (c + 1) * 256)
            bg_ref[rows, cols] = _dot(xb, win_ref[:, c * 256:(c + 1) * 256])
            cg = _dot(xb, win_ref[:, MIX_WIDTH + c * 256:MIX_WIDTH + (c + 1) * 256])
            hv = _dot(xb, win_ref[:, 2 * MIX_WIDTH + c * 256:2 * MIX_WIDTH + (c + 1) * 256])
            ext_ref[SHORT_HALO + r0:SHORT_HALO + r1, cols] = cg * hv
            if c > 0:
                _causal_conv(ext_ref, cw_ref, SHORT_HALO, SHORT_CONV_WIDTH, r0, r1, (c - 1) * 256, c * 256, store)
        qx = _dot(xb, win_ref[:, 3 * MIX_WIDTH:])
        _causal_conv(ext_ref, cw_ref, SHORT_HALO, SHORT_CONV_WIDTH, r0, r1, MIX_WIDTH - 256, MIX_WIDTH, store)
        br_ref[rows, MIX_WIDTH:] = _cross_attention(qx, kbd_ref, vbd_ref).astype(BF16)
        for e0, e1 in _epilogue_blocks(r0, r1, is_last):
            _project_and_norm(x_ref, br_ref, wo_ref, g_ref, b_ref, out_ref, e0, e1)


def _ffn_kernel(x_ref, win_ref, wd_ref, g_ref, b_ref, out_ref, act_ref):
    xb = x_ref[0].astype(BF16)
    for c in range(N_FF_CHUNKS):
        gate = _dot(xb, win_ref[:, c * FF_CHUNK:(c + 1) * FF_CHUNK])
        value = _dot(xb, win_ref[:, D_FF + c * FF_CHUNK:D_FF + (c + 1) * FF_CHUNK])
        act_ref[:, c * FF_CHUNK:(c + 1) * FF_CHUNK] = (jax.nn.silu(gate) * value).astype(BF16)
    for r0, r1, is_last in _sub_blocks():
        for e0, e1 in _epilogue_blocks(r0, r1, is_last):
            _project_and_norm(x_ref, act_ref, wd_ref, g_ref, b_ref, out_ref, e0, e1)


def _const_spec(shape):
    zeros = (0,) * len(shape)
    return pl.BlockSpec(shape, lambda b, s: zeros, pipeline_mode=pl.Buffered(1))


def _per_batch_spec(shape):
    zeros = (0,) * (len(shape) - 1)
    return pl.BlockSpec((1,) + tuple(shape[1:]), lambda b, s: (b,) + zeros)


class _Resident:
    def __init__(self, stacked, index):
        self.stacked = stacked
        self.index = index


def _load_as_bf16(hbm_ref, index, dst_ref, stage_ref, sem):
    n_slots, rows = stage_ref.shape[0], stage_ref.shape[1]
    n_chunks = dst_ref.shape[0] // rows

    def copy(c):
        slot = c % n_slots
        return pltpu.make_async_copy(
            hbm_ref.at[index, pl.ds(c * rows, rows), :], stage_ref.at[slot], sem.at[slot])

    for c in range(min(n_slots - 1, n_chunks)):
        copy(c).start()
    for c in range(n_chunks):
        if c + n_slots - 1 < n_chunks:
            copy(c + n_slots - 1).start()
        copy(c).wait()
        dst_ref[c * rows:(c + 1) * rows, :] = stage_ref[c % n_slots].astype(BF16)


def _stage_rows(rows, cols):
    fits = [d for d in range(BF16_SUBLANES, rows + 1, BF16_SUBLANES)
            if rows % d == 0 and d * cols <= STAGE_ELEMS]
    return max(fits)


def _token_call(body, name, x, consts, per_batch, scratch):
    bsz, seq, _ = x.shape
    x_spec = pl.BlockSpec((1, TILE, D_MODEL), lambda b, s: (b, s, 0))
    operands, specs, resident = [x], [x_spec], []

    def add_const(c):
        if isinstance(c, _Resident):
            resident.append((len(operands), c.index))
            operands.append(c.stacked)
            specs.append(pl.BlockSpec(memory_space=pl.ANY))
        else:
            operands.append(c)
            specs.append(_const_spec(c.shape))

    for c in consts[0]:
        add_const(c)
    for p in per_batch:
        operands.append(p)
        specs.append(_per_batch_spec(p.shape))
    for c in consts[1]:
        add_const(c)

    n_in, n_res = len(operands), len(resident)
    shapes = [operands[pos].shape[1:] for pos, _ in resident]
    resident_scratch = ([pltpu.VMEM(s, BF16) for s in shapes]
                        + [pltpu.VMEM((STAGE_SLOTS, _stage_rows(*s), s[1]), F32) for s in shapes]
                        + [pltpu.SemaphoreType.DMA((STAGE_SLOTS,))])

    def wrapped(*refs):
        in_refs, out_ref, extra = list(refs[:n_in]), refs[n_in], refs[n_in + 1:]
        copies, stages, sem = extra[:n_res], extra[n_res:2 * n_res], extra[2 * n_res]

        @pl.when(jnp.logical_and(pl.program_id(0) == 0, pl.program_id(1) == 0))
        def _():
            for (pos, index), dst, stage in zip(resident, copies, stages):
                _load_as_bf16(in_refs[pos], index, dst, stage, sem)

        for (pos, _), dst in zip(resident, copies):
            in_refs[pos] = dst
        body(*in_refs, out_ref, *extra[2 * n_res + 1:])

    return pl.pallas_call(
        wrapped,
        grid=(bsz, seq // TILE),
        in_specs=specs,
        out_specs=x_spec,
        out_shape=jax.ShapeDtypeStruct(x.shape, F32),
        scratch_shapes=resident_scratch + scratch,
        compiler_params=pltpu.CompilerParams(
            dimension_semantics=("arbitrary", "arbitrary"),
            vmem_limit_bytes=VMEM_LIMIT_BYTES),
        name=name,
    )(*operands)


def _row(v):
    return v.reshape(1, -1)


def kernel(x, mem, a_w_in, a_v_ln_g, a_v_ln_b, a_w_s, a_b_s, b_w_in, b_sinks, rel_bias, c_w_in, c_conv_w, c_conv_b, c_ln_g, c_ln_b, d_w_in, d_conv_w, w_mem_kv, w_o, ln1_g, ln1_b, ffn_w_in, ffn_w_down, ln2_g, ln2_b):
    kbd, vbd = _memory_kv(mem, w_mem_kv)
    branch = pltpu.VMEM((TILE, BRANCH_WIDTH), BF16)
    for i in range(DEPTH):
        m, j = i % 4, i // 4
        tail = [_Resident(w_o, i), _row(ln1_g[i]), _row(ln1_b[i])]
        kv = [kbd[i], vbd[i]]
        if m == 0:
            bias = jnp.broadcast_to(a_b_s[j][:, :, None], (SG_GROUPS, SG_BLOCK, SG_BLOCK))
            head = [_Resident(a_w_in, j), _row(a_v_ln_g[j]), _row(a_v_ln_b[j]), a_w_s[j], bias]
            scratch = [branch, pltpu.VMEM((TILE, MIX_WIDTH), F32), pltpu.VMEM((TILE, MIX_WIDTH), BF16)]
            x = _token_call(_sg_kernel, "mixer_spatial_gating", x, (head, tail), kv, scratch)
        elif m == 1:
            table = _bias_table(rel_bias, b_sinks[j]).reshape(SWA_KV_HEADS, SWA_GROUP * CHUNK, BAND_PAD)
            head = [_Resident(b_w_in, j), table]
            ext = pltpu.VMEM((SWA_KV_HEADS, TILE + BAND - CHUNK, SWA_KV_HEADS * HEAD_DIM), BF16)
            scratch = [branch, pltpu.VMEM((TILE, MIX_WIDTH), BF16), ext, ext]
            x = _token_call(_swa_kernel, "mixer_swa", x, (head, tail), kv, scratch)
        elif m == 2:
            head = [_Resident(c_w_in, j), c_conv_w[j], _row(c_conv_b[j]), _row(c_ln_g[j]), _row(c_ln_b[j])]
            scratch = [branch, pltpu.VMEM((TILE + CONV_HALO, MIX_WIDTH), F32), pltpu.VMEM((TILE, MIX_WIDTH), F32)]
            x = _token_call(_conf_kernel, "mixer_conformer", x, (head, tail), kv, scratch)
        else:
            head = [_Resident(d_w_in, j), d_conv_w[j]]
            scratch = [branch, pltpu.VMEM((TILE + SHORT_HALO, MIX_WIDTH), F32), pltpu.VMEM((TILE, MIX_WIDTH), F32)]
            x = _token_call(_short_kernel, "mixer_short_conv", x, (head, tail), kv, scratch)

        head = [_Resident(ffn_w_in, i), _Resident(ffn_w_down, i), _row(ln2_g[i]), _row(ln2_b[i])]
        x = _token_call(_ffn_kernel, "ffn_swiglu", x, (head, []), [], [pltpu.VMEM((TILE, D_FF), BF16)])
    return x
```

```python
import numpy as np
import jax
import jax.numpy as jnp
from jax import lax
from jax.experimental import pallas as pl
from jax.experimental.pallas import tpu as pltpu

D_MODEL = 1024
DEPTH = 4
CHUNK = 64
HEAD_DIM = 64
MIX_WIDTH = D_MODEL
MEM_LEN = 256
X_HEADS = 4
X_WIDTH = X_HEADS * HEAD_DIM
BRANCH_WIDTH = MIX_WIDTH + X_WIDTH
SG_BLOCK = 128
SG_GROUPS = 8
SWA_KV_HEADS = 2
SWA_GROUP = 8
BAND = 192
BAND_PAD = 256
REL_BUCKETS = 32
REL_MAX_DIST = 128
CONV_WIDTH = 31
SHORT_CONV_WIDTH = 3
D_FF = 2816
FF_CHUNK = 256
N_FF_CHUNKS = D_FF // FF_CHUNK
DEEPNORM_ALPHA = (2 * DEPTH) ** 0.25
LN_EPS = 1e-5
NEG_INF = -1e30
QK_SCALE = HEAD_DIM ** -0.5

TILE = 1024
SWA_UNIT_HEADS = 8
CONV_ROW_BLOCK = 128
SUB_ROWS = TILE
EPILOGUE_ROWS = 256
CONV_HALO = 32
SHORT_HALO = 8
STAGE_ELEMS = 256 * 1024
STAGE_SLOTS = 3
BF16_SUBLANES = 16
VMEM_LIMIT_BYTES = 56 * 1024 * 1024

BF16 = jnp.bfloat16
F32 = jnp.float32


def _dot(a, b):
    return jnp.dot(a, b, preferred_element_type=F32)


def _layer_norm(y, g, b):
    mu = jnp.mean(y, -1, keepdims=True)
    d = y - mu
    var = jnp.mean(d * d, -1, keepdims=True)
    return d * lax.rsqrt(var + LN_EPS) * g + b


def _t5_bucket(rel):
    nb = REL_BUCKETS // 2
    ret = (rel > 0).astype(np.int32) * nb
    n = np.abs(rel)
    max_exact = nb // 2
    large = max_exact + (np.log(np.maximum(n, 1) / max_exact)
                         / np.log(REL_MAX_DIST / max_exact) * (nb - max_exact)).astype(np.int32)
    large = np.minimum(large, nb - 1)
    return (ret + np.where(n < max_exact, n, large)).astype(np.int32)


def _kv_kernel(mem_ref, w_ref, kbd_ref, vbd_ref):
    kv = _dot(mem_ref[0].astype(BF16), w_ref[0].astype(BF16))
    kt = (kv[:, :X_WIDTH] * QK_SCALE).T
    v = kv[:, X_WIDTH:]
    row_head = lax.broadcasted_iota(jnp.int32, (X_WIDTH, MEM_LEN), 0) // HEAD_DIM
    col_head = lax.broadcasted_iota(jnp.int32, (MEM_LEN, X_WIDTH), 1) // HEAD_DIM
    for h in range(X_HEADS):
        kbd_ref[0, 0, :, h * MEM_LEN:(h + 1) * MEM_LEN] = jnp.where(row_head == h, kt, 0.0).astype(BF16)
        vbd_ref[0, 0, h * MEM_LEN:(h + 1) * MEM_LEN, :] = jnp.where(col_head == h, v, 0.0).astype(BF16)


def _memory_kv(mem, w_mem_kv):
    bsz = mem.shape[0]
    return pl.pallas_call(
        _kv_kernel,
        grid=(DEPTH, bsz),
        in_specs=[pl.BlockSpec((1, MEM_LEN, D_MODEL), lambda l, b: (b, 0, 0)),
                  pl.BlockSpec((1, D_MODEL, 2 * X_WIDTH), lambda l, b: (l, 0, 0))],
        out_specs=[pl.BlockSpec((1, 1, X_WIDTH, X_HEADS * MEM_LEN), lambda l, b: (l, b, 0, 0)),
                   pl.BlockSpec((1, 1, X_HEADS * MEM_LEN, X_WIDTH), lambda l, b: (l, b, 0, 0))],
        out_shape=[jax.ShapeDtypeStruct((DEPTH, bsz, X_WIDTH, X_HEADS * MEM_LEN), BF16),
                   jax.ShapeDtypeStruct((DEPTH, bsz, X_HEADS * MEM_LEN, X_WIDTH), BF16)],
        name="memory_kv",
    )(mem, w_mem_kv)


SINK_COLUMN = BAND
_SINK_BUCKET = -2
_PAD_BUCKET = -1


def _bias_kernel(rb_ref, sink_ref, bucket_ref, out_ref):
    h = pl.program_id(0)
    bk = bucket_ref[...]
    acc = jnp.where(bk == _SINK_BUCKET, sink_ref[h], NEG_INF)
    for b in range(REL_BUCKETS):
        acc = jnp.where(bk == b, rb_ref[b, h], acc)
    out_ref[0] = acc


def _bias_table(rel_bias, sinks):
    n_heads = rel_bias.shape[1]
    qpos = np.arange(CHUNK)[:, None]
    kpos = np.arange(BAND)[None, :] - (BAND - CHUNK)
    buckets = np.full((CHUNK, BAND_PAD), _PAD_BUCKET, np.int32)
    buckets[:, :BAND] = _t5_bucket(kpos - qpos)
    buckets[:, SINK_COLUMN] = _SINK_BUCKET
    return pl.pallas_call(
        _bias_kernel,
        grid=(n_heads,),
        in_specs=[pl.BlockSpec(memory_space=pltpu.SMEM),
                  pl.BlockSpec(memory_space=pltpu.SMEM),
                  pl.BlockSpec((CHUNK, BAND_PAD), lambda h: (0, 0))],
        out_specs=pl.BlockSpec((1, CHUNK, BAND_PAD), lambda h: (h, 0, 0)),
        out_shape=jax.ShapeDtypeStruct((n_heads, CHUNK, BAND_PAD), F32),
        name="rel_bias_table",
    )(rel_bias, sinks, jnp.asarray(buckets))


def _cross_attention(qx, kbd_ref, vbd_ref):
    s = _dot(qx.astype(BF16), kbd_ref[0])
    probs = []
    for h in range(X_HEADS):
        sh = s[:, h * MEM_LEN:(h + 1) * MEM_LEN]
        e = jnp.exp(sh - jnp.max(sh, -1, keepdims=True))
        probs.append((e / jnp.sum(e, -1, keepdims=True)).astype(BF16))
    return _dot(jnp.concatenate(probs, -1), vbd_ref[0])


def _project_and_norm(x_ref, act_ref, w_ref, g_ref, b_ref, out_ref, r0, r1):
    rows = slice(r0, r1)
    y = DEEPNORM_ALPHA * x_ref[0, rows, :] + _dot(act_ref[rows, :], w_ref[...])
    out_ref[0, rows, :] = _layer_norm(y, g_ref[...], b_ref[...])


def _epilogue_blocks(r0, r1, is_last):
    bounds = list(range(r0, r1, EPILOGUE_ROWS)) + [r1]
    if is_last:
        bounds.insert(-1, r1 - EPILOGUE_ROWS // 2)
    return list(zip(bounds[:-1], bounds[1:]))


def _sub_blocks():
    return [(r0, r0 + SUB_ROWS, r0 + SUB_ROWS == TILE) for r0 in range(0, TILE, SUB_ROWS)]


def _sg_kernel(x_ref, win_ref, vg_ref, vb_ref, ws_ref, bs_ref, kbd_ref, vbd_ref, wo_ref, g_ref, b_ref,
               out_ref, br_ref, u_ref, v_ref):
    row_chunk = lax.broadcasted_iota(jnp.int32, (SG_BLOCK, SG_BLOCK), 0) // CHUNK
    col_chunk = lax.broadcasted_iota(jnp.int32, (SG_BLOCK, SG_BLOCK), 1) // CHUNK
    gch = MIX_WIDTH // SG_GROUPS
    for r0, r1, is_last in _sub_blocks():
        rows = slice(r0, r1)
        xb = x_ref[0, rows, :].astype(BF16)
        u_ref[rows, :] = jax.nn.gelu(_dot(xb, win_ref[:, :MIX_WIDTH]))
        zv = jax.nn.gelu(_dot(xb, win_ref[:, MIX_WIDTH:2 * MIX_WIDTH]))
        v_ref[rows, :] = _layer_norm(zv, vg_ref[...], vb_ref[...]).astype(BF16)
        qx = _dot(xb, win_ref[:, 2 * MIX_WIDTH:])
        br_ref[rows, MIX_WIDTH:] = _cross_attention(qx, kbd_ref, vbd_ref).astype(BF16)
        zeros = jnp.zeros((SG_BLOCK, gch), BF16)
        for g in range(0, SG_GROUPS, 2):
            w_pair = jnp.concatenate(
                [jnp.where(col_chunk <= row_chunk, ws_ref[g + i], 0.0) for i in range(2)], 1).astype(BF16)
            bias = jnp.concatenate([bs_ref[g], bs_ref[g + 1]], 1)
            lo, hi = slice(g * gch, (g + 1) * gch), slice((g + 1) * gch, (g + 2) * gch)
            cols = slice(g * gch, (g + 2) * gch)
            for n0 in range(r0, r1, SG_BLOCK):
                blk = slice(n0, n0 + SG_BLOCK)
                v_diag = jnp.concatenate([jnp.concatenate([v_ref[blk, lo], zeros], 1),
                                          jnp.concatenate([zeros, v_ref[blk, hi]], 1)], 0)
                sv = _dot(w_pair, v_diag) + bias
                br_ref[blk, cols] = (u_ref[blk, cols] * sv).astype(BF16)
        for e0, e1 in _epilogue_blocks(r0, r1, is_last):
            _project_and_norm(x_ref, br_ref, wo_ref, g_ref, b_ref, out_ref, e0, e1)


def _swa_kernel(x_ref, win_ref, bias_ref, kbd_ref, vbd_ref, wo_ref, g_ref, b_ref,
                out_ref, br_ref, q_ref, k_ref, v_ref):
    s_idx = pl.program_id(1)
    halo = BAND - CHUNK
    kv_w = SWA_KV_HEADS * HEAD_DIM

    @pl.when(s_idx == 0)
    def _():
        k_ref[:, 0:halo, :] = jnp.zeros((SWA_KV_HEADS, halo, kv_w), BF16)
        v_ref[:, 0:halo, :] = jnp.zeros((SWA_KV_HEADS, halo, kv_w), BF16)

    @pl.when(s_idx != 0)
    def _():
        k_ref[:, 0:halo, :] = k_ref[:, TILE:TILE + halo, :]
        v_ref[:, 0:halo, :] = v_ref[:, TILE:TILE + halo, :]

    unit_rows = SWA_UNIT_HEADS * CHUNK
    col = lax.broadcasted_iota(jnp.int32, (unit_rows, BAND_PAD), 1)
    low_q = lax.broadcasted_iota(jnp.int32, (CHUNK, 2 * HEAD_DIM), 1) < HEAD_DIM
    low = lax.broadcasted_iota(jnp.int32, (SUB_ROWS, kv_w), 1) < HEAD_DIM
    pad = BAND_PAD - BAND
    k_tail = jnp.zeros((pad, kv_w), BF16)
    v_ones = jnp.ones((BAND, kv_w), BF16)
    v_tail = jnp.concatenate([jnp.zeros((pad, kv_w), BF16), jnp.ones((pad, kv_w), BF16)], 1)

    for s0, s1, is_last in _sub_blocks():
        rows = slice(s0, s1)
        xb = x_ref[0, rows, :].astype(BF16)
        q_ref[rows, :] = (_dot(xb, win_ref[:, :MIX_WIDTH]) * QK_SCALE).astype(BF16)
        kvx = _dot(xb, win_ref[:, MIX_WIDTH:])
        for dst_ref, t in ((k_ref, kvx[:, :kv_w]), (v_ref, kvx[:, kv_w:2 * kv_w])):
            swapped = pltpu.roll(t, HEAD_DIM, 1)
            dst_ref[0, halo + s0:halo + s1, :] = jnp.where(low, t, swapped).astype(BF16)
            dst_ref[1, halo + s0:halo + s1, :] = jnp.where(low, swapped, t).astype(BF16)
        br_ref[rows, MIX_WIDTH:] = _cross_attention(kvx[:, 2 * kv_w:], kbd_ref, vbd_ref).astype(BF16)

        for c in range(s0 // CHUNK, s1 // CHUNK):
            r0 = c * CHUNK
            for kh in range(SWA_KV_HEADS):
                kb = jnp.concatenate([k_ref[kh, r0:r0 + BAND, :], k_tail], 0)
                vb = jnp.concatenate(
                    [jnp.concatenate([v_ref[kh, r0:r0 + BAND, :], v_ones], 1), v_tail], 0)
                for u in range(SWA_GROUP // SWA_UNIT_HEADS):
                    pairs = range(u * SWA_UNIT_HEADS // 2, (u + 1) * SWA_UNIT_HEADS // 2)
                    pieces = []
                    for j in pairs:
                        qp = q_ref[r0:r0 + CHUNK, kh * 512 + j * 128:kh * 512 + (j + 1) * 128]
                        pieces.append(jnp.where(low_q, qp, jnp.zeros_like(qp)))
                        pieces.append(jnp.where(low_q, jnp.zeros_like(qp), qp))
                    qs = jnp.concatenate(pieces, 0)
                    sc = lax.dot_general(qs, kb, (((1,), (1,)), ((), ())), preferred_element_type=F32)
                    sc = sc + bias_ref[kh, u * unit_rows:(u + 1) * unit_rows, :]
                    if c < BAND // CHUNK - 1:
                        n_invalid = jnp.where(s_idx == 0, (BAND // CHUNK - 1 - c) * CHUNK, 0)
                        sc = jnp.where(col < n_invalid, NEG_INF, sc)
                    e = jnp.exp(sc - jnp.max(sc, -1, keepdims=True)).astype(BF16)
                    o = _dot(e, vb)
                    o = o[:, :kv_w] / o[:, kv_w:]
                    for n, j in enumerate(pairs):
                        lo = o[(2 * n) * CHUNK:(2 * n + 1) * CHUNK]
                        hi = o[(2 * n + 1) * CHUNK:(2 * n + 2) * CHUNK]
                        br_ref[r0:r0 + CHUNK, kh * 512 + j * 128:kh * 512 + (j + 1) * 128] = (
                            jnp.where(low_q, lo, hi).astype(BF16))
        for e0, e1 in _epilogue_blocks(s0, s1, is_last):
            _project_and_norm(x_ref, br_ref, wo_ref, g_ref, b_ref, out_ref, e0, e1)


def _causal_conv(ext_ref, w_ref, halo, width, r_start, r_end, c_start, c_end, store):
    base = halo - (width - 1)
    rb = CONV_ROW_BLOCK
    for cb in range(c_start // 128, c_end // 128):
        cols = slice(cb * 128, (cb + 1) * 128)
        for r0 in range(r_start, r_end, rb):
            acc = None
            for shift in range(8):
                taps = [k for k in range(width) if (base + k) % 8 == shift]
                if not taps:
                    continue
                n_rows = rb if shift == 0 else rb + 8
                part = None
                for k in taps:
                    start = r0 + base + k - shift
                    term = w_ref[k:k + 1, cols] * ext_ref[start:start + n_rows, cols]
                    part = term if part is None else part + term
                if shift:
                    part = pltpu.roll(part, n_rows - shift, 0)[:rb]
                acc = part if acc is None else acc + part
            store(slice(r0, r0 + rb), cols, acc)


def _conf_kernel(x_ref, win_ref, cw_ref, cb_ref, lg_ref, lb_ref, kbd_ref, vbd_ref, wo_ref, g_ref, b_ref,
                 out_ref, br_ref, ext_ref, y_ref):
    s_idx = pl.program_id(1)

    @pl.when(s_idx == 0)
    def _():
        ext_ref[0:CONV_HALO, :] = jnp.zeros((CONV_HALO, MIX_WIDTH), F32)

    @pl.when(s_idx != 0)
    def _():
        ext_ref[0:CONV_HALO, :] = ext_ref[TILE:TILE + CONV_HALO, :]

    def store(rows, cols, acc):
        y_ref[rows, cols] = acc + cb_ref[:, cols]

    for r0, r1, is_last in _sub_blocks():
        rows = slice(r0, r1)
        xb = x_ref[0, rows, :].astype(BF16)
        for c in range(MIX_WIDTH // 256):
            cols = slice(c * 256, (c + 1) * 256)
            a = _dot(xb, win_ref[:, c * 256:(c + 1) * 256])
            gate = _dot(xb, win_ref[:, MIX_WIDTH + c * 256:MIX_WIDTH + (c + 1) * 256])
            ext_ref[CONV_HALO + r0:CONV_HALO + r1, cols] = a * jax.nn.sigmoid(gate)
            if c > 0:
                _causal_conv(ext_ref, cw_ref, CONV_HALO, CONV_WIDTH, r0, r1, (c - 1) * 256, c * 256, store)
        qx = _dot(xb, win_ref[:, 2 * MIX_WIDTH:])
        _causal_conv(ext_ref, cw_ref, CONV_HALO, CONV_WIDTH, r0, r1, MIX_WIDTH - 256, MIX_WIDTH, store)
        br_ref[rows, MIX_WIDTH:] = _cross_attention(qx, kbd_ref, vbd_ref).astype(BF16)
        br_ref[rows, :MIX_WIDTH] = jax.nn.silu(
            _layer_norm(y_ref[rows, :], lg_ref[...], lb_ref[...])).astype(BF16)
        for e0, e1 in _epilogue_blocks(r0, r1, is_last):
            _project_and_norm(x_ref, br_ref, wo_ref, g_ref, b_ref, out_ref, e0, e1)


def _short_kernel(x_ref, win_ref, cw_ref, kbd_ref, vbd_ref, wo_ref, g_ref, b_ref,
                  out_ref, br_ref, ext_ref, bg_ref):
    s_idx = pl.program_id(1)

    @pl.when(s_idx == 0)
    def _():
        ext_ref[0:SHORT_HALO, :] = jnp.zeros((SHORT_HALO, MIX_WIDTH), F32)

    @pl.when(s_idx != 0)
    def _():
        ext_ref[0:SHORT_HALO, :] = ext_ref[TILE:TILE + SHORT_HALO, :]

    def store(rows, cols, acc):
        br_ref[rows, cols] = (bg_ref[rows, cols] * acc).astype(BF16)

    for r0, r1, is_last in _sub_blocks():
        rows = slice(r0, r1)
        xb = x_ref[0, rows, :].astype(BF16)
        for c in range(MIX_WIDTH // 256):
            cols = slice(c * 256, (c + 1) * 256)
            bg_ref[rows, cols] = _dot(xb, win_ref[:, c * 256:(c + 1) * 256])
            cg = _dot(xb, win_ref[:, MIX_WIDTH + c * 256:MIX_WIDTH + (c + 1) * 256])
            hv = _dot(xb, win_ref[:, 2 * MIX_WIDTH + c * 256:2 * MIX_WIDTH + (c + 1) * 256])
            ext_ref[SHORT_HALO + r0:SHORT_HALO + r1, cols] = cg * hv
            if c > 0:
                _causal_conv(ext_ref, cw_ref, SHORT_HALO, SHORT_CONV_WIDTH, r0, r1, (c - 1) * 256, c * 256, store)
        qx = _dot(xb, win_ref[:, 3 * MIX_WIDTH:])
        _causal_conv(ext_ref, cw_ref, SHORT_HALO, SHORT_CONV_WIDTH, r0, r1, MIX_WIDTH - 256, MIX_WIDTH, store)
        br_ref[rows, MIX_WIDTH:] = _cross_attention(qx, kbd_ref, vbd_ref).astype(BF16)
        for e0, e1 in _epilogue_blocks(r0, r1, is_last):
            _project_and_norm(x_ref, br_ref, wo_ref, g_ref, b_ref, out_ref, e0, e1)


def _ffn_kernel(x_ref, win_ref, wd_ref, g_ref, b_ref, out_ref, act_ref):
    xb = x_ref[0].astype(BF16)
    for c in range(N_FF_CHUNKS):
        gate = _dot(xb, win_ref[:, c * FF_CHUNK:(c + 1) * FF_CHUNK])
        value = _dot(xb, win_ref[:, D_FF + c * FF_CHUNK:D_FF + (c + 1) * FF_CHUNK])
        act_ref[:, c * FF_CHUNK:(c + 1) * FF_CHUNK] = (jax.nn.silu(gate) * value).astype(BF16)
    for r0, r1, is_last in _sub_blocks():
        for e0, e1 in _epilogue_blocks(r0, r1, is_last):
            _project_and_norm(x_ref, act_ref, wd_ref, g_ref, b_ref, out_ref, e0, e1)


def _const_spec(shape):
    zeros = (0,) * len(shape)
    return pl.BlockSpec(shape, lambda b, s: zeros, pipeline_mode=pl.Buffered(1))


def _per_batch_spec(shape):
    zeros = (0,) * (len(shape) - 1)
    return pl.BlockSpec((1,) + tuple(shape[1:]), lambda b, s: (b,) + zeros)


class _Resident:
    def __init__(self, stacked, index):
        self.stacked = stacked
        self.index = index


def _load_as_bf16(hbm_ref, index, dst_ref, stage_ref, sem):
    n_slots, rows = stage_ref.shape[0], stage_ref.shape[1]
    n_chunks = dst_ref.shape[0] // rows

    def copy(c):
        slot = c % n_slots
        return pltpu.make_async_copy(
            hbm_ref.at[index, pl.ds(c * rows, rows), :], stage_ref.at[slot], sem.at[slot])

    for c in range(min(n_slots - 1, n_chunks)):
        copy(c).start()
    for c in range(n_chunks):
        if c + n_slots - 1 < n_chunks:
            copy(c + n_slots - 1).start()
        copy(c).wait()
        dst_ref[c * rows:(c + 1) * rows, :] = stage_ref[c % n_slots].astype(BF16)


def _stage_rows(rows, cols):
    fits = [d for d in range(BF16_SUBLANES, rows + 1, BF16_SUBLANES)
            if rows % d == 0 and d * cols <= STAGE_ELEMS]
    return max(fits)


def _token_call(body, name, x, consts, per_batch, scratch):
    bsz, seq, _ = x.shape
    x_spec = pl.BlockSpec((1, TILE, D_MODEL), lambda b, s: (b, s, 0))
    operands, specs, resident = [x], [x_spec], []

    def add_const(c):
        if isinstance(c, _Resident):
            resident.append((len(operands), c.index))
            operands.append(c.stacked)
            specs.append(pl.BlockSpec(memory_space=pl.ANY))
        else:
            operands.append(c)
            specs.append(_const_spec(c.shape))

    for c in consts[0]:
        add_const(c)
    for p in per_batch:
        operands.append(p)
        specs.append(_per_batch_spec(p.shape))
    for c in consts[1]:
        add_const(c)

    n_in, n_res = len(operands), len(resident)
    shapes = [operands[pos].shape[1:] for pos, _ in resident]
    resident_scratch = ([pltpu.VMEM(s, BF16) for s in shapes]
                        + [pltpu.VMEM((STAGE_SLOTS, _stage_rows(*s), s[1]), F32) for s in shapes]
                        + [pltpu.SemaphoreType.DMA((STAGE_SLOTS,))])

    def wrapped(*refs):
        in_refs, out_ref, extra = list(refs[:n_in]), refs[n_in], refs[n_in + 1:]
        copies, stages, sem = extra[:n_res], extra[n_res:2 * n_res], extra[2 * n_res]

        @pl.when(jnp.logical_and(pl.program_id(0) == 0, pl.program_id(1) == 0))
        def _():
            for (pos, index), dst, stage in zip(resident, copies, stages):
                _load_as_bf16(in_refs[pos], index, dst, stage, sem)

        for (pos, _), dst in zip(resident, copies):
            in_refs[pos] = dst
        body(*in_refs, out_ref, *extra[2 * n_res + 1:])

    return pl.pallas_call(
        wrapped,
        grid=(bsz, seq // TILE),
        in_specs=specs,
        out_specs=x_spec,
        out_shape=jax.ShapeDtypeStruct(x.shape, F32),
        scratch_shapes=resident_scratch + scratch,
        compiler_params=pltpu.CompilerParams(
            dimension_semantics=("arbitrary", "arbitrary"),
            vmem_limit_bytes=VMEM_LIMIT_BYTES),
        name=name,
    )(*operands)


def _row(v):
    return v.reshape(1, -1)


def kernel(x, mem, a_w_in, a_v_ln_g, a_v_ln_b, a_w_s, a_b_s, b_w_in, b_sinks, rel_bias, c_w_in, c_conv_w, c_conv_b, c_ln_g, c_ln_b, d_w_in, d_conv_w, w_mem_kv, w_o, ln1_g, ln1_b, ffn_w_in, ffn_w_down, ln2_g, ln2_b):
    kbd, vbd = _memory_kv(mem, w_mem_kv)
    branch = pltpu.VMEM((TILE, BRANCH_WIDTH), BF16)
    for i in range(DEPTH):
        m, j = i % 4, i // 4
        tail = [_Resident(w_o, i), _row(ln1_g[i]), _row(ln1_b[i])]
        kv = [kbd[i], vbd[i]]
        if m == 0:
            bias = jnp.broadcast_to(a_b_s[j][:, :, None], (SG_GROUPS, SG_BLOCK, SG_BLOCK))
            head = [_Resident(a_w_in, j), _row(a_v_ln_g[j]), _row(a_v_ln_b[j]), a_w_s[j], bias]
            scratch = [branch, pltpu.VMEM((TILE, MIX_WIDTH), F32), pltpu.VMEM((TILE, MIX_WIDTH), BF16)]
            x = _token_call(_sg_kernel, "mixer_spatial_gating", x, (head, tail), kv, scratch)
        elif m == 1:
            table = _bias_table(rel_bias, b_sinks[j]).reshape(SWA_KV_HEADS, SWA_GROUP * CHUNK, BAND_PAD)
            head = [_Resident(b_w_in, j), table]
            ext = pltpu.VMEM((SWA_KV_HEADS, TILE + BAND - CHUNK, SWA_KV_HEADS * HEAD_DIM), BF16)
            scratch = [branch, pltpu.VMEM((TILE, MIX_WIDTH), BF16), ext, ext]
            x = _token_call(_swa_kernel, "mixer_swa", x, (head, tail), kv, scratch)
        elif m == 2:
            head = [_Resident(c_w_in, j), c_conv_w[j], _row(c_conv_b[j]), _row(c_ln_g[j]), _row(c_ln_b[j])]
            scratch = [branch, pltpu.VMEM((TILE + CONV_HALO, MIX_WIDTH), F32), pltpu.VMEM((TILE, MIX_WIDTH), F32)]
            x = _token_call(_conf_kernel, "mixer_conformer", x, (head, tail), kv, scratch)
        else:
            head = [_Resident(d_w_in, j), d_conv_w[j]]
            scratch = [branch, pltpu.VMEM((TILE + SHORT_HALO, MIX_WIDTH), F32), pltpu.VMEM((TILE, MIX_WIDTH), F32)]
            x = _token_call(_short_kernel, "mixer_short_conv", x, (head, tail), kv, scratch)

        head = [_Resident(ffn_w_in, i), _Resident(ffn_w_down, i), _row(ln2_g[i]), _row(ln2_b[i])]
        x = _token_call(_ffn_kernel, "ffn_swiglu", x, (head, []), [], [pltpu.VMEM((TILE, D_FF), BF16)])
    return x
```

```python
import numpy as np
import jax
import jax.numpy as jnp
from jax import lax
from jax.experimental import pallas as pl
from jax.experimental.pallas import tpu as pltpu

D_MODEL = 1024
DEPTH = 4
CHUNK = 64
HEAD_DIM = 64
MIX_WIDTH = D_MODEL
MEM_LEN = 256
X_HEADS = 4
X_WIDTH = X_HEADS * HEAD_DIM
BRANCH_WIDTH = MIX_WIDTH + X_WIDTH
SG_BLOCK = 128
SG_GROUPS = 8
SWA_KV_HEADS = 2
SWA_GROUP = 8
BAND = 192
BAND_PAD = 256
REL_BUCKETS = 32
REL_MAX_DIST = 128
CONV_WIDTH = 31
SHORT_CONV_WIDTH = 3
D_FF = 2816
FF_CHUNK = 256
N_FF_CHUNKS = D_FF // FF_CHUNK
DEEPNORM_ALPHA = (2 * DEPTH) ** 0.25
LN_EPS = 1e-5
NEG_INF = -1e30
QK_SCALE = HEAD_DIM ** -0.5

TILE = 1024
SWA_UNIT_HEADS = 8
CONV_ROW_BLOCK = 128
SUB_ROWS = TILE
EPILOGUE_ROWS = 256
CONV_HALO = 32
SHORT_HALO = 8
STAGE_ELEMS = 256 * 1024
STAGE_SLOTS = 3
BF16_SUBLANES = 16
VMEM_LIMIT_BYTES = 56 * 1024 * 1024

BF16 = jnp.bfloat16
F32 = jnp.float32


def _dot(a, b):
    return jnp.dot(a, b, preferred_element_type=F32)


def _layer_norm(y, g, b):
    mu = jnp.mean(y, -1, keepdims=True)
    d = y - mu
    var = jnp.mean(d * d, -1, keepdims=True)
    return d * lax.rsqrt(var + LN_EPS) * g + b


def _t5_bucket(rel):
    nb = REL_BUCKETS // 2
    ret = (rel > 0).astype(np.int32) * nb
    n = np.abs(rel)
    max_exact = nb // 2
    large = max_exact + (np.log(np.maximum(n, 1) / max_exact)
                         / np.log(REL_MAX_DIST / max_exact) * (nb - max_exact)).astype(np.int32)
    large = np.minimum(large, nb - 1)
    return (ret + np.where(n < max_exact, n, large)).astype(np.int32)


def _kv_kernel(mem_ref, w_ref, kbd_ref, vbd_ref):
    kv = _dot(mem_ref[0].astype(BF16), w_ref[0].astype(BF16))
    kt = (kv[:, :X_WIDTH] * QK_SCALE).T
    v = kv[:, X_WIDTH:]
    row_head = lax.broadcasted_iota(jnp.int32, (X_WIDTH, MEM_LEN), 0) // HEAD_DIM
    col_head = lax.broadcasted_iota(jnp.int32, (MEM_LEN, X_WIDTH), 1) // HEAD_DIM
    for h in range(X_HEADS):
        kbd_ref[0, 0, :, h * MEM_LEN:(h + 1) * MEM_LEN] = jnp.where(row_head == h, kt, 0.0).astype(BF16)
        vbd_ref[0, 0, h * MEM_LEN:(h + 1) * MEM_LEN, :] = jnp.where(col_head == h, v, 0.0).astype(BF16)


def _memory_kv(mem, w_mem_kv):
    bsz = mem.shape[0]
    return pl.pallas_call(
        _kv_kernel,
        grid=(DEPTH, bsz),
        in_specs=[pl.BlockSpec((1, MEM_LEN, D_MODEL), lambda l, b: (b, 0, 0)),
                  pl.BlockSpec((1, D_MODEL, 2 * X_WIDTH), lambda l, b: (l, 0, 0))],
        out_specs=[pl.BlockSpec((1, 1, X_WIDTH, X_HEADS * MEM_LEN), lambda l, b: (l, b, 0, 0)),
                   pl.BlockSpec((1, 1, X_HEADS * MEM_LEN, X_WIDTH), lambda l, b: (l, b, 0, 0))],
        out_shape=[jax.ShapeDtypeStruct((DEPTH, bsz, X_WIDTH, X_HEADS * MEM_LEN), BF16),
                   jax.ShapeDtypeStruct((DEPTH, bsz, X_HEADS * MEM_LEN, X_WIDTH), BF16)],
        name="memory_kv",
    )(mem, w_mem_kv)


SINK_COLUMN = BAND
_SINK_BUCKET = -2
_PAD_BUCKET = -1


def _bias_kernel(rb_ref, sink_ref, bucket_ref, out_ref):
    h = pl.program_id(0)
    bk = bucket_ref[...]
    acc = jnp.where(bk == _SINK_BUCKET, sink_ref[h], NEG_INF)
    for b in range(REL_BUCKETS):
        acc = jnp.where(bk == b, rb_ref[b, h], acc)
    out_ref[0] = acc


def _bias_table(rel_bias, sinks):
    n_heads = rel_bias.shape[1]
    qpos = np.arange(CHUNK)[:, None]
    kpos = np.arange(BAND)[None, :] - (BAND - CHUNK)
    buckets = np.full((CHUNK, BAND_PAD), _PAD_BUCKET, np.int32)
    buckets[:, :BAND] = _t5_bucket(kpos - qpos)
    buckets[:, SINK_COLUMN] = _SINK_BUCKET
    return pl.pallas_call(
        _bias_kernel,
        grid=(n_heads,),
        in_specs=[pl.BlockSpec(memory_space=pltpu.SMEM),
                  pl.BlockSpec(memory_space=pltpu.SMEM),
                  pl.BlockSpec((CHUNK, BAND_PAD), lambda h: (0, 0))],
        out_specs=pl.BlockSpec((1, CHUNK, BAND_PAD), lambda h: (h, 0, 0)),
        out_shape=jax.ShapeDtypeStruct((n_heads, CHUNK, BAND_PAD), F32),
        name="rel_bias_table",
    )(rel_bias, sinks, jnp.asarray(buckets))


def _cross_attention(qx, kbd_ref, vbd_ref):
    s = _dot(qx.astype(BF16), kbd_ref[0])
    probs = []
    for h in range(X_HEADS):
        sh = s[:, h * MEM_LEN:(h + 1) * MEM_LEN]
        e = jnp.exp(sh - jnp.max(sh, -1, keepdims=True))
        probs.append((e / jnp.sum(e, -1, keepdims=True)).astype(BF16))
    return _dot(jnp.concatenate(probs, -1), vbd_ref[0])


def _project_and_norm(x_ref, act_ref, w_ref, g_ref, b_ref, out_ref, r0, r1):
    rows = slice(r0, r1)
    y = DEEPNORM_ALPHA * x_ref[0, rows, :] + _dot(act_ref[rows, :], w_ref[...])
    out_ref[0, rows, :] = _layer_norm(y, g_ref[...], b_ref[...])


def _epilogue_blocks(r0, r1, is_last):
    bounds = list(range(r0, r1, EPILOGUE_ROWS)) + [r1]
    if is_last:
        bounds.insert(-1, r1 - EPILOGUE_ROWS // 2)
    return list(zip(bounds[:-1], bounds[1:]))


def _sub_blocks():
    return [(r0, r0 + SUB_ROWS, r0 + SUB_ROWS == TILE) for r0 in range(0, TILE, SUB_ROWS)]


def _sg_kernel(x_ref, win_ref, vg_ref, vb_ref, ws_ref, bs_ref, kbd_ref, vbd_ref, wo_ref, g_ref, b_ref,
               out_ref, br_ref, u_ref, v_ref):
    row_chunk = lax.broadcasted_iota(jnp.int32, (SG_BLOCK, SG_BLOCK), 0) // CHUNK
    col_chunk = lax.broadcasted_iota(jnp.int32, (SG_BLOCK, SG_BLOCK), 1) // CHUNK
    gch = MIX_WIDTH // SG_GROUPS
    for r0, r1, is_last in _sub_blocks():
        rows = slice(r0, r1)
        xb = x_ref[0, rows, :].astype(BF16)
        u_ref[rows, :] = jax.nn.gelu(_dot(xb, win_ref[:, :MIX_WIDTH]))
        zv = jax.nn.gelu(_dot(xb, win_ref[:, MIX_WIDTH:2 * MIX_WIDTH]))
        v_ref[rows, :] = _layer_norm(zv, vg_ref[...], vb_ref[...]).astype(BF16)
        qx = _dot(xb, win_ref[:, 2 * MIX_WIDTH:])
        br_ref[rows, MIX_WIDTH:] = _cross_attention(qx, kbd_ref, vbd_ref).astype(BF16)
        for g in range(SG_GROUPS):
            w_m = jnp.where(col_chunk <= row_chunk, ws_ref[g], 0.0).astype(BF16)
            cols = slice(g * gch, (g + 1) * gch)
            for n0 in range(r0, r1, SG_BLOCK):
                blk = slice(n0, n0 + SG_BLOCK)
                sv = _dot(w_m, v_ref[blk, cols]) + bs_ref[g]
                br_ref[blk, cols] = (u_ref[blk, cols] * sv).astype(BF16)
        for e0, e1 in _epilogue_blocks(r0, r1, is_last):
            _project_and_norm(x_ref, br_ref, wo_ref, g_ref, b_ref, out_ref, e0, e1)


def _swa_kernel(x_ref, win_ref, bias_ref, kbd_ref, vbd_ref, wo_ref, g_ref, b_ref,
                out_ref, br_ref, q_ref, k_ref, v_ref):
    s_idx = pl.program_id(1)
    halo = BAND - CHUNK
    kv_w = SWA_KV_HEADS * HEAD_DIM

    @pl.when(s_idx == 0)
    def _():
        k_ref[:, 0:halo, :] = jnp.zeros((SWA_KV_HEADS, halo, kv_w), BF16)
        v_ref[:, 0:halo, :] = jnp.zeros((SWA_KV_HEADS, halo, kv_w), BF16)

    @pl.when(s_idx != 0)
    def _():
        k_ref[:, 0:halo, :] = k_ref[:, TILE:TILE + halo, :]
        v_ref[:, 0:halo, :] = v_ref[:, TILE:TILE + halo, :]

    unit_rows = SWA_UNIT_HEADS * CHUNK
    col = lax.broadcasted_iota(jnp.int32, (unit_rows, BAND_PAD), 1)
    low_q = lax.broadcasted_iota(jnp.int32, (CHUNK, 2 * HEAD_DIM), 1) < HEAD_DIM
    low = lax.broadcasted_iota(jnp.int32, (SUB_ROWS, kv_w), 1) < HEAD_DIM
    pad = BAND_PAD - BAND
    k_tail = jnp.zeros((pad, kv_w), BF16)
    v_ones = jnp.ones((BAND, kv_w), BF16)
    v_tail = jnp.concatenate([jnp.zeros((pad, kv_w), BF16), jnp.ones((pad, kv_w), BF16)], 1)

    for s0, s1, is_last in _sub_blocks():
        rows = slice(s0, s1)
        xb = x_ref[0, rows, :].astype(BF16)
        q_ref[rows, :] = (_dot(xb, win_ref[:, :MIX_WIDTH]) * QK_SCALE).astype(BF16)
        kvx = _dot(xb, win_ref[:, MIX_WIDTH:])
        for dst_ref, t in ((k_ref, kvx[:, :kv_w]), (v_ref, kvx[:, kv_w:2 * kv_w])):
            swapped = pltpu.roll(t, HEAD_DIM, 1)
            dst_ref[0, halo + s0:halo + s1, :] = jnp.where(low, t, swapped).astype(BF16)
            dst_ref[1, halo + s0:halo + s1, :] = jnp.where(low, swapped, t).astype(BF16)
        br_ref[rows, MIX_WIDTH:] = _cross_attention(kvx[:, 2 * kv_w:], kbd_ref, vbd_ref).astype(BF16)

        for c in range(s0 // CHUNK, s1 // CHUNK):
            r0 = c * CHUNK
            for kh in range(SWA_KV_HEADS):
                kb = jnp.concatenate([k_ref[kh, r0:r0 + BAND, :], k_tail], 0)
                vb = jnp.concatenate(
                    [jnp.concatenate([v_ref[kh, r0:r0 + BAND, :], v_ones], 1), v_tail], 0)
                for u in range(SWA_GROUP // SWA_UNIT_HEADS):
                    pairs = range(u * SWA_UNIT_HEADS // 2, (u + 1) * SWA_UNIT_HEADS // 2)
                    pieces = []
                    for j in pairs:
                        qp = q_ref[r0:r0 + CHUNK, kh * 512 + j * 128:kh * 512 + (j + 1) * 128]
                        pieces.append(jnp.where(low_q, qp, jnp.zeros_like(qp)))
                        pieces.append(jnp.where(low_q, jnp.zeros_like(qp), qp))
                    qs = jnp.concatenate(pieces, 0)
                    sc = lax.dot_general(qs, kb, (((1,), (1,)), ((), ())), preferred_element_type=F32)
                    sc = sc + bias_ref[kh, u * unit_rows:(u + 1) * unit_rows, :]
                    if c < BAND // CHUNK - 1:
                        n_invalid = jnp.where(s_idx == 0, (BAND // CHUNK - 1 - c) * CHUNK, 0)
                        sc = jnp.where(col < n_invalid, NEG_INF, sc)
                    e = jnp.exp(sc - jnp.max(sc, -1, keepdims=True)).astype(BF16)
                    o = _dot(e, vb)
                    o = o[:, :kv_w] / o[:, kv_w:]
                    for n, j in enumerate(pairs):
                        lo = o[(2 * n) * CHUNK:(2 * n + 1) * CHUNK]
                        hi = o[(2 * n + 1) * CHUNK:(2 * n + 2) * CHUNK]
                        br_ref[r0:r0 + CHUNK, kh * 512 + j * 128:kh * 512 + (j + 1) * 128] = (
                            jnp.where(low_q, lo, hi).astype(BF16))
        for e0, e1 in _epilogue_blocks(s0, s1, is_last):
            _project_and_norm(x_ref, br_ref, wo_ref, g_ref, b_ref, out_ref, e0, e1)


def _causal_conv(ext_ref, w_ref, halo, width, r_start, r_end, c_start, c_end, store):
    base = halo - (width - 1)
    rb = CONV_ROW_BLOCK
    for cb in range(c_start // 128, c_end // 128):
        cols = slice(cb * 128, (cb + 1) * 128)
        for r0 in range(r_start, r_end, rb):
            acc = None
            for shift in range(8):
                taps = [k for k in range(width) if (base + k) % 8 == shift]
                if not taps:
                    continue
                n_rows = rb if shift == 0 else rb + 8
                part = None
                for k in taps:
                    start = r0 + base + k - shift
                    term = w_ref[k:k + 1, cols] * ext_ref[start:start + n_rows, cols]
                    part = term if part is None else part + term
                if shift:
                    part = pltpu.roll(part, n_rows - shift, 0)[:rb]
                acc = part if acc is None else acc + part
            store(slice(r0, r0 + rb), cols, acc)


def _conf_kernel(x_ref, win_ref, cw_ref, cb_ref, lg_ref, lb_ref, kbd_ref, vbd_ref, wo_ref, g_ref, b_ref,
                 out_ref, br_ref, ext_ref, y_ref):
    s_idx = pl.program_id(1)

    @pl.when(s_idx == 0)
    def _():
        ext_ref[0:CONV_HALO, :] = jnp.zeros((CONV_HALO, MIX_WIDTH), F32)

    @pl.when(s_idx != 0)
    def _():
        ext_ref[0:CONV_HALO, :] = ext_ref[TILE:TILE + CONV_HALO, :]

    def store(rows, cols, acc):
        y_ref[rows, cols] = acc + cb_ref[:, cols]

    for r0, r1, is_last in _sub_blocks():
        rows = slice(r0, r1)
        xb = x_ref[0, rows, :].astype(BF16)
        for c in range(MIX_WIDTH // 256):
            cols = slice(c * 256, (c + 1) * 256)
            a = _dot(xb, win_ref[:, c * 256:(c + 1) * 256])
            gate = _dot(xb, win_ref[:, MIX_WIDTH + c * 256:MIX_WIDTH + (c + 1) * 256])
            ext_ref[CONV_HALO + r0:CONV_HALO + r1, cols] = a * jax.nn.sigmoid(gate)
            if c > 0:
                _causal_conv(ext_ref, cw_ref, CONV_HALO, CONV_WIDTH, r0, r1, (c - 1) * 256, c * 256, store)
        qx = _dot(xb, win_ref[:, 2 * MIX_WIDTH:])
        _causal_conv(ext_ref, cw_ref, CONV_HALO, CONV_WIDTH, r0, r1, MIX_WIDTH - 256, MIX_WIDTH, store)
        br_ref[rows, MIX_WIDTH:] = _cross_attention(qx, kbd_ref, vbd_ref).astype(BF16)
        br_ref[rows, :MIX_WIDTH] = jax.nn.silu(
            _layer_norm(y_ref[rows, :], lg_ref[...], lb_ref[...])).astype(BF16)
        for e0, e1 in _epilogue_blocks(r0, r1, is_last):
            _project_and_norm(x_ref, br_ref, wo_ref, g_ref, b_ref, out_ref, e0, e1)


def _short_kernel(x_ref, win_ref, cw_ref, kbd_ref, vbd_ref, wo_ref, g_ref, b_ref,
                  out_ref, br_ref, ext_ref, bg_ref):
    s_idx = pl.program_id(1)

    @pl.when(s_idx == 0)
    def _():
        ext_ref[0:SHORT_HALO, :] = jnp.zeros((SHORT_HALO, MIX_WIDTH), F32)

    @pl.when(s_idx != 0)
    def _():
        ext_ref[0:SHORT_HALO, :] = ext_ref[TILE:TILE + SHORT_HALO, :]

    def store(rows, cols, acc):
        br_ref[rows, cols] = (bg_ref[rows, cols] * acc).astype(BF16)

    for r0, r1, is_last in _sub_blocks():
        rows = slice(r0, r1)
        xb = x_ref[0, rows, :].astype(BF16)
        for c in range(MIX_WIDTH // 256):
            cols = slice(c * 256, (c + 1) * 256)
            bg_ref[rows, cols] = _dot(xb, win_ref[:, c * 256:(c + 1) * 256])
            cg = _dot(xb, win_ref[:, MIX_WIDTH + c * 256:MIX_WIDTH + (c + 1) * 256])
            hv = _dot(xb, win_ref[:, 2 * MIX_WIDTH + c * 256:2 * MIX_WIDTH + (c + 1) * 256])
            ext_ref[SHORT_HALO + r0:SHORT_HALO + r1, cols] = cg * hv
            if c > 0:
                _causal_conv(ext_ref, cw_ref, SHORT_HALO, SHORT_CONV_WIDTH, r0, r1, (c - 1) * 256, c * 256, store)
        qx = _dot(xb, win_ref[:, 3 * MIX_WIDTH:])
        _causal_conv(ext_ref, cw_ref, SHORT_HALO, SHORT_CONV_WIDTH, r0, r1, MIX_WIDTH - 256, MIX_WIDTH, store)
        br_ref[rows, MIX_WIDTH:] = _cross_attention(qx, kbd_ref, vbd_ref).astype(BF16)
        for e0, e1 in _epilogue_blocks(r0, r1, is_last):
            _project_and_norm(x_ref, br_ref, wo_ref, g_ref, b_ref, out_ref, e0, e1)


def _ffn_kernel(x_ref, win_ref, wd_ref, g_ref, b_ref, out_ref, act_ref):
    xb = x_ref[0].astype(BF16)
    for c in range(N_FF_CHUNKS):
        gate = _dot(xb, win_ref[:, c * FF_CHUNK:(c + 1) * FF_CHUNK])
        value = _dot(xb, win_ref[:, D_FF + c * FF_CHUNK:D_FF + (c + 1) * FF_CHUNK])
        act_ref[:, c * FF_CHUNK:(c + 1) * FF_CHUNK] = (jax.nn.silu(gate) * value).astype(BF16)
    for r0, r1, is_last in _sub_blocks():
        for e0, e1 in _epilogue_blocks(r0, r1, is_last):
            _project_and_norm(x_ref, act_ref, wd_ref, g_ref, b_ref, out_ref, e0, e1)


def _const_spec(shape):
    zeros = (0,) * len(shape)
    return pl.BlockSpec(shape, lambda b, s: zeros, pipeline_mode=pl.Buffered(1))


def _per_batch_spec(shape):
    zeros = (0,) * (len(shape) - 1)
    return pl.BlockSpec((1,) + tuple(shape[1:]), lambda b, s: (b,) + zeros)


class _Resident:
    def __init__(self, stacked, index):
        self.stacked = stacked
        self.index = index


def _load_as_bf16(hbm_ref, index, dst_ref, stage_ref, sem):
    n_slots, rows = stage_ref.shape[0], stage_ref.shape[1]
    n_chunks = dst_ref.shape[0] // rows

    def copy(c):
        slot = c % n_slots
        return pltpu.make_async_copy(
            hbm_ref.at[index, pl.ds(c * rows, rows), :], stage_ref.at[slot], sem.at[slot])

    for c in range(min(n_slots - 1, n_chunks)):
        copy(c).start()
    for c in range(n_chunks):
        if c + n_slots - 1 < n_chunks:
            copy(c + n_slots - 1).start()
        copy(c).wait()
        dst_ref[c * rows:(c + 1) * rows, :] = stage_ref[c % n_slots].astype(BF16)


def _stage_rows(rows, cols):
    fits = [d for d in range(BF16_SUBLANES, rows + 1, BF16_SUBLANES)
            if rows % d == 0 and d * cols <= STAGE_ELEMS]
    return max(fits)


def _token_call(body, name, x, consts, per_batch, scratch):
    bsz, seq, _ = x.shape
    x_spec = pl.BlockSpec((1, TILE, D_MODEL), lambda b, s: (b, s, 0))
    operands, specs, resident = [x], [x_spec], []

    def add_const(c):
        if isinstance(c, _Resident):
            resident.append((len(operands), c.index))
            operands.append(c.stacked)
            specs.append(pl.BlockSpec(memory_space=pl.ANY))
        else:
            operands.append(c)
            specs.append(_const_spec(c.shape))

    for c in consts[0]:
        add_const(c)
    for p in per_batch:
        operands.append(p)
        specs.append(_per_batch_spec(p.shape))
    for c in consts[1]:
        add_const(c)

    n_in, n_res = len(operands), len(resident)
    shapes = [operands[pos].shape[1:] for pos, _ in resident]
    resident_scratch = ([pltpu.VMEM(s, BF16) for s in shapes]
                        + [pltpu.VMEM((STAGE_SLOTS, _stage_rows(*s), s[1]), F32) for s in shapes]
                        + [pltpu.SemaphoreType.DMA((STAGE_SLOTS,))])

    def wrapped(*refs):
        in_refs, out_ref, extra = list(refs[:n_in]), refs[n_in], refs[n_in + 1:]
        copies, stages, sem = extra[:n_res], extra[n_res:2 * n_res], extra[2 * n_res]

        @pl.when(jnp.logical_and(pl.program_id(0) == 0, pl.program_id(1) == 0))
        def _():
            for (pos, index), dst, stage in zip(resident, copies, stages):
                _load_as_bf16(in_refs[pos], index, dst, stage, sem)

        for (pos, _), dst in zip(resident, copies):
            in_refs[pos] = dst
        body(*in_refs, out_ref, *extra[2 * n_res + 1:])

    return pl.pallas_call(
        wrapped,
        grid=(bsz, seq // TILE),
        in_specs=specs,
        out_specs=x_spec,
        out_shape=jax.ShapeDtypeStruct(x.shape, F32),
        scratch_shapes=resident_scratch + scratch,
        compiler_params=pltpu.CompilerParams(
            dimension_semantics=("arbitrary", "arbitrary"),
            vmem_limit_bytes=VMEM_LIMIT_BYTES),
        name=name,
    )(*operands)


def _row(v):
    return v.reshape(1, -1)


def kernel(x, mem, a_w_in, a_v_ln_g, a_v_ln_b, a_w_s, a_b_s, b_w_in, b_sinks, rel_bias, c_w_in, c_conv_w, c_conv_b, c_ln_g, c_ln_b, d_w_in, d_conv_w, w_mem_kv, w_o, ln1_g, ln1_b, ffn_w_in, ffn_w_down, ln2_g, ln2_b):
    kbd, vbd = _memory_kv(mem, w_mem_kv)
    branch = pltpu.VMEM((TILE, BRANCH_WIDTH), BF16)
    for i in range(DEPTH):
        m, j = i % 4, i // 4
        tail = [_Resident(w_o, i), _row(ln1_g[i]), _row(ln1_b[i])]
        kv = [kbd[i], vbd[i]]
        if m == 0:
            bias = jnp.broadcast_to(a_b_s[j][:, :, None], (SG_GROUPS, SG_BLOCK, SG_BLOCK))
            head = [_Resident(a_w_in, j), _row(a_v_ln_g[j]), _row(a_v_ln_b[j]), a_w_s[j], bias]
            scratch = [branch, pltpu.VMEM((TILE, MIX_WIDTH), F32), pltpu.VMEM((TILE, MIX_WIDTH), BF16)]
            x = _token_call(_sg_kernel, "mixer_spatial_gating", x, (head, tail), kv, scratch)
        elif m == 1:
            table = _bias_table(rel_bias, b_sinks[j]).reshape(SWA_KV_HEADS, SWA_GROUP * CHUNK, BAND_PAD)
            head = [_Resident(b_w_in, j), table]
            ext = pltpu.VMEM((SWA_KV_HEADS, TILE + BAND - CHUNK, SWA_KV_HEADS * HEAD_DIM), BF16)
            scratch = [branch, pltpu.VMEM((TILE, MIX_WIDTH), BF16), ext, ext]
            x = _token_call(_swa_kernel, "mixer_swa", x, (head, tail), kv, scratch)
        elif m == 2:
            head = [_Resident(c_w_in, j), c_conv_w[j], _row(c_conv_b[j]), _row(c_ln_g[j]), _row(c_ln_b[j])]
            scratch = [branch, pltpu.VMEM((TILE + CONV_HALO, MIX_WIDTH), F32), pltpu.VMEM((TILE, MIX_WIDTH), F32)]
            x = _token_call(_conf_kernel, "mixer_conformer", x, (head, tail), kv, scratch)
        else:
            head = [_Resident(d_w_in, j), d_conv_w[j]]
            scratch = [branch, pltpu.VMEM((TILE + SHORT_HALO, MIX_WIDTH), F32), pltpu.VMEM((TILE, MIX_WIDTH), F32)]
            x = _token_call(_short_kernel, "mixer_short_conv", x, (head, tail), kv, scratch)

        head = [_Resident(ffn_w_in, i), _Resident(ffn_w_down, i), _row(ln2_g[i]), _row(ln2_b[i])]
        x = _token_call(_ffn_kernel, "ffn_swiglu", x, (head, []), [], [pltpu.VMEM((TILE, D_FF), BF16)])
    return x
```

```python
import numpy as np
import jax
import jax.numpy as jnp
from jax import lax
from jax.experimental import pallas as pl
from jax.experimental.pallas import tpu as pltpu

D_MODEL = 1024
DEPTH = 4
CHUNK = 64
HEAD_DIM = 64
MIX_WIDTH = D_MODEL
MEM_LEN = 256
X_HEADS = 4
X_WIDTH = X_HEADS * HEAD_DIM
BRANCH_WIDTH = MIX_WIDTH + X_WIDTH
SG_BLOCK = 128
SG_GROUPS = 8
SWA_KV_HEADS = 2
SWA_GROUP = 8
BAND = 192
BAND_PAD = 256
REL_BUCKETS = 32
REL_MAX_DIST = 128
CONV_WIDTH = 31
SHORT_CONV_WIDTH = 3
D_FF = 2816
FF_CHUNK = 256
N_FF_CHUNKS = D_FF // FF_CHUNK
DEEPNORM_ALPHA = (2 * DEPTH) ** 0.25
LN_EPS = 1e-5
NEG_INF = -1e30
QK_SCALE = HEAD_DIM ** -0.5

TILE = 1024
SWA_UNIT_HEADS = 8
CONV_ROW_BLOCK = 128
SUB_ROWS = TILE
EPILOGUE_ROWS = 256
EPILOGUE_MIN_ROWS = 128
CONV_HALO = 32
SHORT_HALO = 8
STAGE_ELEMS = 256 * 1024
STAGE_SLOTS = 3
BF16_SUBLANES = 16
VMEM_LIMIT_BYTES = 56 * 1024 * 1024

BF16 = jnp.bfloat16
F32 = jnp.float32


def _dot(a, b):
    return jnp.dot(a, b, preferred_element_type=F32)


def _layer_norm(y, g, b):
    mu = jnp.mean(y, -1, keepdims=True)
    d = y - mu
    var = jnp.mean(d * d, -1, keepdims=True)
    return d * lax.rsqrt(var + LN_EPS) * g + b


def _t5_bucket(rel):
    nb = REL_BUCKETS // 2
    ret = (rel > 0).astype(np.int32) * nb
    n = np.abs(rel)
    max_exact = nb // 2
    large = max_exact + (np.log(np.maximum(n, 1) / max_exact)
                         / np.log(REL_MAX_DIST / max_exact) * (nb - max_exact)).astype(np.int32)
    large = np.minimum(large, nb - 1)
    return (ret + np.where(n < max_exact, n, large)).astype(np.int32)


def _kv_kernel(mem_ref, w_ref, kbd_ref, vbd_ref):
    kv = _dot(mem_ref[0].astype(BF16), w_ref[0].astype(BF16))
    kt = (kv[:, :X_WIDTH] * QK_SCALE).T
    v = kv[:, X_WIDTH:]
    row_head = lax.broadcasted_iota(jnp.int32, (X_WIDTH, MEM_LEN), 0) // HEAD_DIM
    col_head = lax.broadcasted_iota(jnp.int32, (MEM_LEN, X_WIDTH), 1) // HEAD_DIM
    for h in range(X_HEADS):
        kbd_ref[0, 0, :, h * MEM_LEN:(h + 1) * MEM_LEN] = jnp.where(row_head == h, kt, 0.0).astype(BF16)
        vbd_ref[0, 0, h * MEM_LEN:(h + 1) * MEM_LEN, :] = jnp.where(col_head == h, v, 0.0).astype(BF16)


def _memory_kv(mem, w_mem_kv):
    bsz = mem.shape[0]
    return pl.pallas_call(
        _kv_kernel,
        grid=(DEPTH, bsz),
        in_specs=[pl.BlockSpec((1, MEM_LEN, D_MODEL), lambda l, b: (b, 0, 0)),
                  pl.BlockSpec((1, D_MODEL, 2 * X_WIDTH), lambda l, b: (l, 0, 0))],
        out_specs=[pl.BlockSpec((1, 1, X_WIDTH, X_HEADS * MEM_LEN), lambda l, b: (l, b, 0, 0)),
                   pl.BlockSpec((1, 1, X_HEADS * MEM_LEN, X_WIDTH), lambda l, b: (l, b, 0, 0))],
        out_shape=[jax.ShapeDtypeStruct((DEPTH, bsz, X_WIDTH, X_HEADS * MEM_LEN), BF16),
                   jax.ShapeDtypeStruct((DEPTH, bsz, X_HEADS * MEM_LEN, X_WIDTH), BF16)],
        name="memory_kv",
    )(mem, w_mem_kv)


SINK_COLUMN = BAND
_SINK_BUCKET = -2
_PAD_BUCKET = -1


def _bias_kernel(rb_ref, sink_ref, bucket_ref, out_ref):
    h = pl.program_id(0)
    bk = bucket_ref[...]
    acc = jnp.where(bk == _SINK_BUCKET, sink_ref[h], NEG_INF)
    for b in range(REL_BUCKETS):
        acc = jnp.where(bk == b, rb_ref[b, h], acc)
    out_ref[0] = acc


def _bias_table(rel_bias, sinks):
    n_heads = rel_bias.shape[1]
    qpos = np.arange(CHUNK)[:, None]
    kpos = np.arange(BAND)[None, :] - (BAND - CHUNK)
    buckets = np.full((CHUNK, BAND_PAD), _PAD_BUCKET, np.int32)
    buckets[:, :BAND] = _t5_bucket(kpos - qpos)
    buckets[:, SINK_COLUMN] = _SINK_BUCKET
    return pl.pallas_call(
        _bias_kernel,
        grid=(n_heads,),
        in_specs=[pl.BlockSpec(memory_space=pltpu.SMEM),
                  pl.BlockSpec(memory_space=pltpu.SMEM),
                  pl.BlockSpec((CHUNK, BAND_PAD), lambda h: (0, 0))],
        out_specs=pl.BlockSpec((1, CHUNK, BAND_PAD), lambda h: (h, 0, 0)),
        out_shape=jax.ShapeDtypeStruct((n_heads, CHUNK, BAND_PAD), F32),
        name="rel_bias_table",
    )(rel_bias, sinks, jnp.asarray(buckets))


def _cross_attention(qx, kbd_ref, vbd_ref):
    s = _dot(qx.astype(BF16), kbd_ref[0])
    probs = []
    for h in range(X_HEADS):
        sh = s[:, h * MEM_LEN:(h + 1) * MEM_LEN]
        e = jnp.exp(sh - jnp.max(sh, -1, keepdims=True))
        probs.append((e / jnp.sum(e, -1, keepdims=True)).astype(BF16))
    return _dot(jnp.concatenate(probs, -1), vbd_ref[0])


def _project_and_norm(x_ref, act_ref, w_ref, g_ref, b_ref, out_ref, r0, r1):
    rows = slice(r0, r1)
    y = DEEPNORM_ALPHA * x_ref[0, rows, :] + _dot(act_ref[rows, :], w_ref[...])
    out_ref[0, rows, :] = _layer_norm(y, g_ref[...], b_ref[...])


def _epilogue_blocks(r0, r1, is_last):
    if not is_last:
        bounds = list(range(r0, r1, EPILOGUE_ROWS)) + [r1]
    else:
        bounds, size = [r0], (r1 - r0) // 2
        while size >= EPILOGUE_MIN_ROWS:
            bounds.append(bounds[-1] + size)
            size //= 2
        bounds.append(r1)
    return list(zip(bounds[:-1], bounds[1:]))


def _sub_blocks():
    return [(r0, r0 + SUB_ROWS, r0 + SUB_ROWS == TILE) for r0 in range(0, TILE, SUB_ROWS)]


def _sg_kernel(x_ref, win_ref, vg_ref, vb_ref, ws_ref, bs_ref, kbd_ref, vbd_ref, wo_ref, g_ref, b_ref,
               out_ref, br_ref, u_ref, v_ref):
    row_chunk = lax.broadcasted_iota(jnp.int32, (SG_BLOCK, SG_BLOCK), 0) // CHUNK
    col_chunk = lax.broadcasted_iota(jnp.int32, (SG_BLOCK, SG_BLOCK), 1) // CHUNK
    gch = MIX_WIDTH // SG_GROUPS
    for r0, r1, is_last in _sub_blocks():
        rows = slice(r0, r1)
        xb = x_ref[0, rows, :].astype(BF16)
        u_ref[rows, :] = jax.nn.gelu(_dot(xb, win_ref[:, :MIX_WIDTH]))
        zv = jax.nn.gelu(_dot(xb, win_ref[:, MIX_WIDTH:2 * MIX_WIDTH]))
        v_ref[rows, :] = _layer_norm(zv, vg_ref[...], vb_ref[...]).astype(BF16)
        qx = _dot(xb, win_ref[:, 2 * MIX_WIDTH:])
        br_ref[rows, MIX_WIDTH:] = _cross_attention(qx, kbd_ref, vbd_ref).astype(BF16)
        for g in range(SG_GROUPS):
            w_m = jnp.where(col_chunk <= row_chunk, ws_ref[g], 0.0).astype(BF16)
            cols = slice(g * gch, (g + 1) * gch)
            for n0 in range(r0, r1, SG_BLOCK):
                blk = slice(n0, n0 + SG_BLOCK)
                sv = _dot(w_m, v_ref[blk, cols]) + bs_ref[g]
                br_ref[blk, cols] = (u_ref[blk, cols] * sv).astype(BF16)
        for e0, e1 in _epilogue_blocks(r0, r1, is_last):
            _project_and_norm(x_ref, br_ref, wo_ref, g_ref, b_ref, out_ref, e0, e1)


def _swa_kernel(x_ref, win_ref, bias_ref, kbd_ref, vbd_ref, wo_ref, g_ref, b_ref,
                out_ref, br_ref, q_ref, k_ref, v_ref):
    s_idx = pl.program_id(1)
    halo = BAND - CHUNK
    kv_w = SWA_KV_HEADS * HEAD_DIM

    @pl.when(s_idx == 0)
    def _():
        k_ref[:, 0:halo, :] = jnp.zeros((SWA_KV_HEADS, halo, kv_w), BF16)
        v_ref[:, 0:halo, :] = jnp.zeros((SWA_KV_HEADS, halo, kv_w), BF16)

    @pl.when(s_idx != 0)
    def _():
        k_ref[:, 0:halo, :] = k_ref[:, TILE:TILE + halo, :]
        v_ref[:, 0:halo, :] = v_ref[:, TILE:TILE + halo, :]

    unit_rows = SWA_UNIT_HEADS * CHUNK
    col = lax.broadcasted_iota(jnp.int32, (unit_rows, BAND_PAD), 1)
    low_q = lax.broadcasted_iota(jnp.int32, (CHUNK, 2 * HEAD_DIM), 1) < HEAD_DIM
    low = lax.broadcasted_iota(jnp.int32, (SUB_ROWS, kv_w), 1) < HEAD_DIM
    pad = BAND_PAD - BAND
    k_tail = jnp.zeros((pad, kv_w), BF16)
    v_ones = jnp.ones((BAND, kv_w), BF16)
    v_tail = jnp.concatenate([jnp.zeros((pad, kv_w), BF16), jnp.ones((pad, kv_w), BF16)], 1)

    for s0, s1, is_last in _sub_blocks():
        rows = slice(s0, s1)
        xb = x_ref[0, rows, :].astype(BF16)
        q_ref[rows, :] = (_dot(xb, win_ref[:, :MIX_WIDTH]) * QK_SCALE).astype(BF16)
        kvx = _dot(xb, win_ref[:, MIX_WIDTH:])
        for dst_ref, t in ((k_ref, kvx[:, :kv_w]), (v_ref, kvx[:, kv_w:2 * kv_w])):
            swapped = pltpu.roll(t, HEAD_DIM, 1)
            dst_ref[0, halo + s0:halo + s1, :] = jnp.where(low, t, swapped).astype(BF16)
            dst_ref[1, halo + s0:halo + s1, :] = jnp.where(low, swapped, t).astype(BF16)
        br_ref[rows, MIX_WIDTH:] = _cross_attention(kvx[:, 2 * kv_w:], kbd_ref, vbd_ref).astype(BF16)

        for c in range(s0 // CHUNK, s1 // CHUNK):
            r0 = c * CHUNK
            for kh in range(SWA_KV_HEADS):
                kb = jnp.concatenate([k_ref[kh, r0:r0 + BAND, :], k_tail], 0)
                vb = jnp.concatenate(
                    [jnp.concatenate([v_ref[kh, r0:r0 + BAND, :], v_ones], 1), v_tail], 0)
                for u in range(SWA_GROUP // SWA_UNIT_HEADS):
                    pairs = range(u * SWA_UNIT_HEADS // 2, (u + 1) * SWA_UNIT_HEADS // 2)
                    pieces = []
                    for j in pairs:
                        qp = q_ref[r0:r0 + CHUNK, kh * 512 + j * 128:kh * 512 + (j + 1) * 128]
                        pieces.append(jnp.where(low_q, qp, jnp.zeros_like(qp)))
                        pieces.append(jnp.where(low_q, jnp.zeros_like(qp), qp))
                    qs = jnp.concatenate(pieces, 0)
                    sc = lax.dot_general(qs, kb, (((1,), (1,)), ((), ())), preferred_element_type=F32)
                    sc = sc + bias_ref[kh, u * unit_rows:(u + 1) * unit_rows, :]
                    if c < BAND // CHUNK - 1:
                        n_invalid = jnp.where(s_idx == 0, (BAND // CHUNK - 1 - c) * CHUNK, 0)
                        sc = jnp.where(col < n_invalid, NEG_INF, sc)
                    e = jnp.exp(sc - jnp.max(sc, -1, keepdims=True)).astype(BF16)
                    o = _dot(e, vb)
                    o = o[:, :kv_w] / o[:, kv_w:]
                    for n, j in enumerate(pairs):
                        lo = o[(2 * n) * CHUNK:(2 * n + 1) * CHUNK]
                        hi = o[(2 * n + 1) * CHUNK:(2 * n + 2) * CHUNK]
                        br_ref[r0:r0 + CHUNK, kh * 512 + j * 128:kh * 512 + (j + 1) * 128] = (
                            jnp.where(low_q, lo, hi).astype(BF16))
        for e0, e1 in _epilogue_blocks(s0, s1, is_last):
            _project_and_norm(x_ref, br_ref, wo_ref, g_ref, b_ref, out_ref, e0, e1)


def _causal_conv(ext_ref, w_ref, halo, width, r_start, r_end, c_start, c_end, store):
    base = halo - (width - 1)
    rb = CONV_ROW_BLOCK
    for cb in range(c_start // 128, c_end // 128):
        cols = slice(cb * 128, (cb + 1) * 128)
        for r0 in range(r_start, r_end, rb):
            acc = None
            for shift in range(8):
                taps = [k for k in range(width) if (base + k) % 8 == shift]
                if not taps:
                    continue
                n_rows = rb if shift == 0 else rb + 8
                part = None
                for k in taps:
                    start = r0 + base + k - shift
                    term = w_ref[k:k + 1, cols] * ext_ref[start:start + n_rows, cols]
                    part = term if part is None else part + term
                if shift:
                    part = pltpu.roll(part, n_rows - shift, 0)[:rb]
                acc = part if acc is None else acc + part
            store(slice(r0, r0 + rb), cols, acc)


def _conf_kernel(x_ref, win_ref, cw_ref, cb_ref, lg_ref, lb_ref, kbd_ref, vbd_ref, wo_ref, g_ref, b_ref,
                 out_ref, br_ref, ext_ref, y_ref):
    s_idx = pl.program_id(1)

    @pl.when(s_idx == 0)
    def _():
        ext_ref[0:CONV_HALO, :] = jnp.zeros((CONV_HALO, MIX_WIDTH), F32)

    @pl.when(s_idx != 0)
    def _():
        ext_ref[0:CONV_HALO, :] = ext_ref[TILE:TILE + CONV_HALO, :]

    def store(rows, cols, acc):
        y_ref[rows, cols] = acc + cb_ref[:, cols]

    for r0, r1, is_last in _sub_blocks():
        rows = slice(r0, r1)
        xb = x_ref[0, rows, :].astype(BF16)
        for c in range(MIX_WIDTH // 256):
            cols = slice(c * 256, (c + 1) * 256)
            a = _dot(xb, win_ref[:, c * 256:(c + 1) * 256])
            gate = _dot(xb, win_ref[:, MIX_WIDTH + c * 256:MIX_WIDTH + (c + 1) * 256])
            ext_ref[CONV_HALO + r0:CONV_HALO + r1, cols] = a * jax.nn.sigmoid(gate)
            if c > 0:
                _causal_conv(ext_ref, cw_ref, CONV_HALO, CONV_WIDTH, r0, r1, (c - 1) * 256, c * 256, store)
        qx = _dot(xb, win_ref[:, 2 * MIX_WIDTH:])
        _causal_conv(ext_ref, cw_ref, CONV_HALO, CONV_WIDTH, r0, r1, MIX_WIDTH - 256, MIX_WIDTH, store)
        br_ref[rows, MIX_WIDTH:] = _cross_attention(qx, kbd_ref, vbd_ref).astype(BF16)
        br_ref[rows, :MIX_WIDTH] = jax.nn.silu(
            _layer_norm(y_ref[rows, :], lg_ref[...], lb_ref[...])).astype(BF16)
        for e0, e1 in _epilogue_blocks(r0, r1, is_last):
            _project_and_norm(x_ref, br_ref, wo_ref, g_ref, b_ref, out_ref, e0, e1)


def _short_kernel(x_ref, win_ref, cw_ref, kbd_ref, vbd_ref, wo_ref, g_ref, b_ref,
                  out_ref, br_ref, ext_ref, bg_ref):
    s_idx = pl.program_id(1)

    @pl.when(s_idx == 0)
    def _():
        ext_ref[0:SHORT_HALO, :] = jnp.zeros((SHORT_HALO, MIX_WIDTH), F32)

    @pl.when(s_idx != 0)
    def _():
        ext_ref[0:SHORT_HALO, :] = ext_ref[TILE:TILE + SHORT_HALO, :]

    def store(rows, cols, acc):
        br_ref[rows, cols] = (bg_ref[rows, cols] * acc).astype(BF16)

    for r0, r1, is_last in _sub_blocks():
        rows = slice(r0, r1)
        xb = x_ref[0, rows, :].astype(BF16)
        for c in range(MIX_WIDTH // 256):
            cols = slice(c * 256, (c + 1) * 256)
            bg_ref[rows, cols] = _dot(xb, win_ref[:, c * 256:(c + 1) * 256])
            cg = _dot(xb, win_ref[:, MIX_WIDTH + c * 256:MIX_WIDTH + (c + 1) * 256])
            hv = _dot(xb, win_ref[:, 2 * MIX_WIDTH + c * 256:2 * MIX_WIDTH + (c + 1) * 256])
            ext_ref[SHORT_HALO + r0:SHORT_HALO + r1, cols] = cg * hv
            if c > 0:
                _causal_conv(ext_ref, cw_ref, SHORT_HALO, SHORT_CONV_WIDTH, r0, r1, (c - 1) * 256, c * 256, store)
        qx = _dot(xb, win_ref[:, 3 * MIX_WIDTH:])
        _causal_conv(ext_ref, cw_ref, SHORT_HALO, SHORT_CONV_WIDTH, r0, r1, MIX_WIDTH - 256, MIX_WIDTH, store)
        br_ref[rows, MIX_WIDTH:] = _cross_attention(qx, kbd_ref, vbd_ref).astype(BF16)
        for e0, e1 in _epilogue_blocks(r0, r1, is_last):
            _project_and_norm(x_ref, br_ref, wo_ref, g_ref, b_ref, out_ref, e0, e1)


def _ffn_kernel(x_ref, win_ref, wd_ref, g_ref, b_ref, out_ref, act_ref):
    xb = x_ref[0].astype(BF16)
    for c in range(N_FF_CHUNKS):
        gate = _dot(xb, win_ref[:, c * FF_CHUNK:(c + 1) * FF_CHUNK])
        value = _dot(xb, win_ref[:, D_FF + c * FF_CHUNK:D_FF + (c + 1) * FF_CHUNK])
        act_ref[:, c * FF_CHUNK:(c + 1) * FF_CHUNK] = (jax.nn.silu(gate) * value).astype(BF16)
    for r0, r1, is_last in _sub_blocks():
        for e0, e1 in _epilogue_blocks(r0, r1, is_last):
            _project_and_norm(x_ref, act_ref, wd_ref, g_ref, b_ref, out_ref, e0, e1)


def _const_spec(shape):
    zeros = (0,) * len(shape)
    return pl.BlockSpec(shape, lambda b, s: zeros, pipeline_mode=pl.Buffered(1))


def _per_batch_spec(shape):
    zeros = (0,) * (len(shape) - 1)
    return pl.BlockSpec((1,) + tuple(shape[1:]), lambda b, s: (b,) + zeros)


class _Resident:
    def __init__(self, stacked, index):
        self.stacked = stacked
        self.index = index


def _load_as_bf16(hbm_ref, index, dst_ref, stage_ref, sem):
    n_slots, rows = stage_ref.shape[0], stage_ref.shape[1]
    n_chunks = dst_ref.shape[0] // rows

    def copy(c):
        slot = c % n_slots
        return pltpu.make_async_copy(
            hbm_ref.at[index, pl.ds(c * rows, rows), :], stage_ref.at[slot], sem.at[slot])

    for c in range(min(n_slots - 1, n_chunks)):
        copy(c).start()
    for c in range(n_chunks):
        if c + n_slots - 1 < n_chunks:
            copy(c + n_slots - 1).start()
        copy(c).wait()
        dst_ref[c * rows:(c + 1) * rows, :] = stage_ref[c % n_slots].astype(BF16)


def _stage_rows(rows, cols):
    fits = [d for d in range(BF16_SUBLANES, rows + 1, BF16_SUBLANES)
            if rows % d == 0 and d * cols <= STAGE_ELEMS]
    return max(fits)


def _token_call(body, name, x, consts, per_batch, scratch):
    bsz, seq, _ = x.shape
    x_spec = pl.BlockSpec((1, TILE, D_MODEL), lambda b, s: (b, s, 0))
    operands, specs, resident = [x], [x_spec], []

    def add_const(c):
        if isinstance(c, _Resident):
            resident.append((len(operands), c.index))
            operands.append(c.stacked)
            specs.append(pl.BlockSpec(memory_space=pl.ANY))
        else:
            operands.append(c)
            specs.append(_const_spec(c.shape))

    for c in consts[0]:
        add_const(c)
    for p in per_batch:
        operands.append(p)
        specs.append(_per_batch_spec(p.shape))
    for c in consts[1]:
        add_const(c)

    n_in, n_res = len(operands), len(resident)
    shapes = [operands[pos].shape[1:] for pos, _ in resident]
    resident_scratch = ([pltpu.VMEM(s, BF16) for s in shapes]
                        + [pltpu.VMEM((STAGE_SLOTS, _stage_rows(*s), s[1]), F32) for s in shapes]
                        + [pltpu.SemaphoreType.DMA((STAGE_SLOTS,))])

    def wrapped(*refs):
        in_refs, out_ref, extra = list(refs[:n_in]), refs[n_in], refs[n_in + 1:]
        copies, stages, sem = extra[:n_res], extra[n_res:2 * n_res], extra[2 * n_res]

        @pl.when(jnp.logical_and(pl.program_id(0) == 0, pl.program_id(1) == 0))
        def _():
            for (pos, index), dst, stage in zip(resident, copies, stages):
                _load_as_bf16(in_refs[pos], index, dst, stage, sem)

        for (pos, _), dst in zip(resident, copies):
            in_refs[pos] = dst
        body(*in_refs, out_ref, *extra[2 * n_res + 1:])

    return pl.pallas_call(
        wrapped,
        grid=(bsz, seq // TILE),
        in_specs=specs,
        out_specs=x_spec,
        out_shape=jax.ShapeDtypeStruct(x.shape, F32),
        scratch_shapes=resident_scratch + scratch,
        compiler_params=pltpu.CompilerParams(
            dimension_semantics=("arbitrary", "arbitrary"),
            vmem_limit_bytes=VMEM_LIMIT_BYTES),
        name=name,
    )(*operands)


def _row(v):
    return v.reshape(1, -1)


def kernel(x, mem, a_w_in, a_v_ln_g, a_v_ln_b, a_w_s, a_b_s, b_w_in, b_sinks, rel_bias, c_w_in, c_conv_w, c_conv_b, c_ln_g, c_ln_b, d_w_in, d_conv_w, w_mem_kv, w_o, ln1_g, ln1_b, ffn_w_in, ffn_w_down, ln2_g, ln2_b):
    kbd, vbd = _memory_kv(mem, w_mem_kv)
    branch = pltpu.VMEM((TILE, BRANCH_WIDTH), BF16)
    for i in range(DEPTH):
        m, j = i % 4, i // 4
        tail = [_Resident(w_o, i), _row(ln1_g[i]), _row(ln1_b[i])]
        kv = [kbd[i], vbd[i]]
        if m == 0:
            bias = jnp.broadcast_to(a_b_s[j][:, :, None], (SG_GROUPS, SG_BLOCK, SG_BLOCK))
            head = [_Resident(a_w_in, j), _row(a_v_ln_g[j]), _row(a_v_ln_b[j]), a_w_s[j], bias]
            scratch = [branch, pltpu.VMEM((TILE, MIX_WIDTH), F32), pltpu.VMEM((TILE, MIX_WIDTH), BF16)]
            x = _token_call(_sg_kernel, "mixer_spatial_gating", x, (head, tail), kv, scratch)
        elif m == 1:
            table = _bias_table(rel_bias, b_sinks[j]).reshape(SWA_KV_HEADS, SWA_GROUP * CHUNK, BAND_PAD)
            head = [_Resident(b_w_in, j), table]
            ext = pltpu.VMEM((SWA_KV_HEADS, TILE + BAND - CHUNK, SWA_KV_HEADS * HEAD_DIM), BF16)
            scratch = [branch, pltpu.VMEM((TILE, MIX_WIDTH), BF16), ext, ext]
            x = _token_call(_swa_kernel, "mixer_swa", x, (head, tail), kv, scratch)
        elif m == 2:
            head = [_Resident(c_w_in, j), c_conv_w[j], _row(c_conv_b[j]), _row(c_ln_g[j]), _row(c_ln_b[j])]
            scratch = [branch, pltpu.VMEM((TILE + CONV_HALO, MIX_WIDTH), F32), pltpu.VMEM((TILE, MIX_WIDTH), F32)]
            x = _token_call(_conf_kernel, "mixer_conformer", x, (head, tail), kv, scratch)
        else:
            head = [_Resident(d_w_in, j), d_conv_w[j]]
            scratch = [branch, pltpu.VMEM((TILE + SHORT_HALO, MIX_WIDTH), F32), pltpu.VMEM((TILE, MIX_WIDTH), F32)]
            x = _token_call(_short_kernel, "mixer_short_conv", x, (head, tail), kv, scratch)

        head = [_Resident(ffn_w_in, i), _Resident(ffn_w_down, i), _row(ln2_g[i]), _row(ln2_b[i])]
        x = _token_call(_ffn_kernel, "ffn_swiglu", x, (head, []), [], [pltpu.VMEM((TILE, D_FF), BF16)])
    return x
```

```python
import numpy as np
import jax
import jax.numpy as jnp
from jax import lax
from jax.experimental import pallas as pl
from jax.experimental.pallas import tpu as pltpu

D_MODEL = 1024
DEPTH = 4
CHUNK = 64
HEAD_DIM = 64
MIX_WIDTH = D_MODEL
MEM_LEN = 256
X_HEADS = 4
X_WIDTH = X_HEADS * HEAD_DIM
BRANCH_WIDTH = MIX_WIDTH + X_WIDTH
SG_BLOCK = 128
SG_GROUPS = 8
SWA_KV_HEADS = 2
SWA_GROUP = 8
BAND = 192
BAND_PAD = 256
REL_BUCKETS = 32
REL_MAX_DIST = 128
CONV_WIDTH = 31
SHORT_CONV_WIDTH = 3
D_FF = 2816
FF_CHUNK = 256
N_FF_CHUNKS = D_FF // FF_CHUNK
DEEPNORM_ALPHA = (2 * DEPTH) ** 0.25
LN_EPS = 1e-5
NEG_INF = -1e30
QK_SCALE = HEAD_DIM ** -0.5

TILE = 1024
SWA_UNIT_HEADS = 8
CONV_ROW_BLOCK = 128
SUB_ROWS = TILE
EPILOGUE_ROWS = 256
CONV_HALO = 32
SHORT_HALO = 8
STAGE_ELEMS = 256 * 1024
STAGE_SLOTS = 3
BF16_SUBLANES = 16
VMEM_LIMIT_BYTES = 56 * 1024 * 1024

BF16 = jnp.bfloat16
F32 = jnp.float32


def _dot(a, b):
    return jnp.dot(a, b, preferred_element_type=F32)


def _layer_norm(y, g, b):
    mu = jnp.mean(y, -1, keepdims=True)
    d = y - mu
    var = jnp.mean(d * d, -1, keepdims=True)
    return d * lax.rsqrt(var + LN_EPS) * g + b


def _t5_bucket(rel):
    nb = REL_BUCKETS // 2
    ret = (rel > 0).astype(np.int32) * nb
    n = np.abs(rel)
    max_exact = nb // 2
    large = max_exact + (np.log(np.maximum(n, 1) / max_exact)
                         / np.log(REL_MAX_DIST / max_exact) * (nb - max_exact)).astype(np.int32)
    large = np.minimum(large, nb - 1)
    return (ret + np.where(n < max_exact, n, large)).astype(np.int32)


def _kv_kernel(mem_ref, w_ref, kbd_ref, vbd_ref):
    kv = _dot(mem_ref[0].astype(BF16), w_ref[0].astype(BF16))
    kt = (kv[:, :X_WIDTH] * QK_SCALE).T
    v = kv[:, X_WIDTH:]
    row_head = lax.broadcasted_iota(jnp.int32, (X_WIDTH, MEM_LEN), 0) // HEAD_DIM
    col_head = lax.broadcasted_iota(jnp.int32, (MEM_LEN, X_WIDTH), 1) // HEAD_DIM
    for h in range(X_HEADS):
        kbd_ref[0, 0, :, h * MEM_LEN:(h + 1) * MEM_LEN] = jnp.where(row_head == h, kt, 0.0).astype(BF16)
        vbd_ref[0, 0, h * MEM_LEN:(h + 1) * MEM_LEN, :] = jnp.where(col_head == h, v, 0.0).astype(BF16)


def _memory_kv(mem, w_mem_kv):
    bsz = mem.shape[0]
    return pl.pallas_call(
        _kv_kernel,
        grid=(DEPTH, bsz),
        in_specs=[pl.BlockSpec((1, MEM_LEN, D_MODEL), lambda l, b: (b, 0, 0)),
                  pl.BlockSpec((1, D_MODEL, 2 * X_WIDTH), lambda l, b: (l, 0, 0))],
        out_specs=[pl.BlockSpec((1, 1, X_WIDTH, X_HEADS * MEM_LEN), lambda l, b: (l, b, 0, 0)),
                   pl.BlockSpec((1, 1, X_HEADS * MEM_LEN, X_WIDTH), lambda l, b: (l, b, 0, 0))],
        out_shape=[jax.ShapeDtypeStruct((DEPTH, bsz, X_WIDTH, X_HEADS * MEM_LEN), BF16),
                   jax.ShapeDtypeStruct((DEPTH, bsz, X_HEADS * MEM_LEN, X_WIDTH), BF16)],
        name="memory_kv",
    )(mem, w_mem_kv)


SINK_COLUMN = BAND
_SINK_BUCKET = -2
_PAD_BUCKET = -1


def _bias_kernel(rb_ref, sink_ref, bucket_ref, out_ref):
    h = pl.program_id(0)
    bk = bucket_ref[...]
    acc = jnp.where(bk == _SINK_BUCKET, sink_ref[h], NEG_INF)
    for b in range(REL_BUCKETS):
        acc = jnp.where(bk == b, rb_ref[b, h], acc)
    out_ref[0] = acc


def _bias_table(rel_bias, sinks):
    n_heads = rel_bias.shape[1]
    qpos = np.arange(CHUNK)[:, None]
    kpos = np.arange(BAND)[None, :] - (BAND - CHUNK)
    buckets = np.full((CHUNK, BAND_PAD), _PAD_BUCKET, np.int32)
    buckets[:, :BAND] = _t5_bucket(kpos - qpos)
    buckets[:, SINK_COLUMN] = _SINK_BUCKET
    return pl.pallas_call(
        _bias_kernel,
        grid=(n_heads,),
        in_specs=[pl.BlockSpec(memory_space=pltpu.SMEM),
                  pl.BlockSpec(memory_space=pltpu.SMEM),
                  pl.BlockSpec((CHUNK, BAND_PAD), lambda h: (0, 0))],
        out_specs=pl.BlockSpec((1, CHUNK, BAND_PAD), lambda h: (h, 0, 0)),
        out_shape=jax.ShapeDtypeStruct((n_heads, CHUNK, BAND_PAD), F32),
        name="rel_bias_table",
    )(rel_bias, sinks, jnp.asarray(buckets))


def _cross_attention(qx, kbd_ref, vbd_ref):
    s = _dot(qx.astype(BF16), kbd_ref[0])
    probs = []
    for h in range(X_HEADS):
        sh = s[:, h * MEM_LEN:(h + 1) * MEM_LEN]
        e = jnp.exp(sh - jnp.max(sh, -1, keepdims=True))
        probs.append((e / jnp.sum(e, -1, keepdims=True)).astype(BF16))
    return _dot(jnp.concatenate(probs, -1), vbd_ref[0])


def _project_and_norm(x_ref, act_ref, w_ref, g_ref, b_ref, out_ref, r0, r1):
    rows = slice(r0, r1)
    y = DEEPNORM_ALPHA * x_ref[0, rows, :] + _dot(act_ref[rows, :], w_ref[...])
    out_ref[0, rows, :] = _layer_norm(y, g_ref[...], b_ref[...])


def _epilogue_blocks(r0, r1, is_last):
    bounds = list(range(r0, r1, EPILOGUE_ROWS)) + [r1]
    if is_last:
        bounds.insert(-1, r1 - EPILOGUE_ROWS // 2)
    return list(zip(bounds[:-1], bounds[1:]))


def _sub_blocks():
    return [(r0, r0 + SUB_ROWS, r0 + SUB_ROWS == TILE) for r0 in range(0, TILE, SUB_ROWS)]


def _sg_kernel(x_ref, win_ref, vg_ref, vb_ref, ws_ref, bs_ref, kbd_ref, vbd_ref, wo_ref, g_ref, b_ref,
               out_ref, br_ref, u_ref, v_ref):
    row_chunk = lax.broadcasted_iota(jnp.int32, (SG_BLOCK, SG_BLOCK), 0) // CHUNK
    col_chunk = lax.broadcasted_iota(jnp.int32, (SG_BLOCK, SG_BLOCK), 1) // CHUNK
    gch = MIX_WIDTH // SG_GROUPS
    for r0, r1, is_last in _sub_blocks():
        rows = slice(r0, r1)
        xb = x_ref[0, rows, :].astype(BF16)
        u_ref[rows, :] = jax.nn.gelu(_dot(xb, win_ref[:, :MIX_WIDTH]))
        zv = jax.nn.gelu(_dot(xb, win_ref[:, MIX_WIDTH:2 * MIX_WIDTH]))
        v_ref[rows, :] = _layer_norm(zv, vg_ref[...], vb_ref[...]).astype(BF16)
        qx = _dot(xb, win_ref[:, 2 * MIX_WIDTH:])
        br_ref[rows, MIX_WIDTH:] = _cross_attention(qx, kbd_ref, vbd_ref).astype(BF16)
        for g in range(SG_GROUPS):
            w_m = jnp.where(col_chunk <= row_chunk, ws_ref[g], 0.0).astype(BF16)
            cols = slice(g * gch, (g + 1) * gch)
            for n0 in range(r0, r1, SG_BLOCK):
                blk = slice(n0, n0 + SG_BLOCK)
                sv = _dot(w_m, v_ref[blk, cols]) + bs_ref[g]
                br_ref[blk, cols] = (u_ref[blk, cols] * sv).astype(BF16)
        for e0, e1 in _epilogue_blocks(r0, r1, is_last):
            _project_and_norm(x_ref, br_ref, wo_ref, g_ref, b_ref, out_ref, e0, e1)


def _swa_kernel(x_ref, win_ref, bias_ref, kbd_ref, vbd_ref, wo_ref, g_ref, b_ref,
                out_ref, br_ref, q_ref, k_ref, v_ref):
    s_idx = pl.program_id(1)
    halo = BAND - CHUNK
    kv_w = SWA_KV_HEADS * HEAD_DIM

    @pl.when(s_idx == 0)
    def _():
        k_ref[:, 0:halo, :] = jnp.zeros((SWA_KV_HEADS, halo, kv_w), BF16)
        v_ref[:, 0:halo, :] = jnp.zeros((SWA_KV_HEADS, halo, kv_w), BF16)

    @pl.when(s_idx != 0)
    def _():
        k_ref[:, 0:halo, :] = k_ref[:, TILE:TILE + halo, :]
        v_ref[:, 0:halo, :] = v_ref[:, TILE:TILE + halo, :]

    unit_rows = SWA_UNIT_HEADS * CHUNK
    col = lax.broadcasted_iota(jnp.int32, (unit_rows, BAND_PAD), 1)
    low_q = lax.broadcasted_iota(jnp.int32, (CHUNK, 2 * HEAD_DIM), 1) < HEAD_DIM
    low = lax.broadcasted_iota(jnp.int32, (SUB_ROWS, kv_w), 1) < HEAD_DIM
    pad = BAND_PAD - BAND
    k_tail = jnp.zeros((pad, kv_w), BF16)
    v_ones = jnp.ones((BAND, kv_w), BF16)
    v_tail = jnp.concatenate([jnp.zeros((pad, kv_w), BF16), jnp.ones((pad, kv_w), BF16)], 1)

    for s0, s1, is_last in _sub_blocks():
        rows = slice(s0, s1)
        xb = x_ref[0, rows, :].astype(BF16)
        q_ref[rows, :] = (_dot(xb, win_ref[:, :MIX_WIDTH]) * QK_SCALE).astype(BF16)
        kvx = _dot(xb, win_ref[:, MIX_WIDTH:])
        for dst_ref, t in ((k_ref, kvx[:, :kv_w]), (v_ref, kvx[:, kv_w:2 * kv_w])):
            swapped = pltpu.roll(t, HEAD_DIM, 1)
            dst_ref[0, halo + s0:halo + s1, :] = jnp.where(low, t, swapped).astype(BF16)
            dst_ref[1, halo + s0:halo + s1, :] = jnp.where(low, swapped, t).astype(BF16)
        br_ref[rows, MIX_WIDTH:] = _cross_attention(kvx[:, 2 * kv_w:], kbd_ref, vbd_ref).astype(BF16)

        for c in range(s0 // CHUNK, s1 // CHUNK):
            r0 = c * CHUNK
            for kh in range(SWA_KV_HEADS):
                kb = jnp.concatenate([k_ref[kh, r0:r0 + BAND, :], k_tail], 0)
                vb = jnp.concatenate(
                    [jnp.concatenate([v_ref[kh, r0:r0 + BAND, :], v_ones], 1), v_tail], 0)
                for u in range(SWA_GROUP // SWA_UNIT_HEADS):
                    pairs = range(u * SWA_UNIT_HEADS // 2, (u + 1) * SWA_UNIT_HEADS // 2)
                    pieces = []
                    for j in pairs:
                        qp = q_ref[r0:r0 + CHUNK, kh * 512 + j * 128:kh * 512 + (j + 1) * 128]
                        pieces.append(jnp.where(low_q, qp, jnp.zeros_like(qp)))
                        pieces.append(jnp.where(low_q, jnp.zeros_like(qp), qp))
                    qs = jnp.concatenate(pieces, 0)
                    sc = lax.dot_general(qs, kb, (((1,), (1,)), ((), ())), preferred_element_type=F32)
                    sc = sc + bias_ref[kh, u * unit_rows:(u + 1) * unit_rows, :]
                    if c < BAND // CHUNK - 1:
                        n_invalid = jnp.where(s_idx == 0, (BAND // CHUNK - 1 - c) * CHUNK, 0)
                        sc = jnp.where(col < n_invalid, NEG_INF, sc)
                    e = jnp.exp(sc - jnp.max(sc, -1, keepdims=True)).astype(BF16)
                    o = _dot(e, vb)
                    o = o[:, :kv_w] / o[:, kv_w:]
                    for n, j in enumerate(pairs):
                        lo = o[(2 * n) * CHUNK:(2 * n + 1) * CHUNK]
                        hi = o[(2 * n + 1) * CHUNK:(2 * n + 2) * CHUNK]
                        br_ref[r0:r0 + CHUNK, kh * 512 + j * 128:kh * 512 + (j + 1) * 128] = (
                            jnp.where(low_q, lo, hi).astype(BF16))
        for e0, e1 in _epilogue_blocks(s0, s1, is_last):
            _project_and_norm(x_ref, br_ref, wo_ref, g_ref, b_ref, out_ref, e0, e1)


def _causal_conv(ext_ref, w_ref, halo, width, r_start, r_end, c_start, c_end, store):
    base = halo - (width - 1)
    rb = CONV_ROW_BLOCK
    for cb in range(c_start // 128, c_end // 128):
        cols = slice(cb * 128, (cb + 1) * 128)
        for r0 in range(r_start, r_end, rb):
            acc = None
            for shift in range(8):
                taps = [k for k in range(width) if (base + k) % 8 == shift]
                if not taps:
                    continue
                n_rows = rb if shift == 0 else rb + 8
                part = None
                for k in taps:
                    start = r0 + base + k - shift
                    term = w_ref[k:k + 1, cols] * ext_ref[start:start + n_rows, cols]
                    part = term if part is None else part + term
                if shift:
                    part = pltpu.roll(part, n_rows - shift, 0)[:rb]
                acc = part if acc is None else acc + part
            store(slice(r0, r0 + rb), cols, acc)


def _conf_kernel(x_ref, win_ref, cw_ref, cb_ref, lg_ref, lb_ref, kbd_ref, vbd_ref, wo_ref, g_ref, b_ref,
                 out_ref, br_ref, ext_ref, y_ref):
    s_idx = pl.program_id(1)

    @pl.when(s_idx == 0)
    def _():
        ext_ref[0:CONV_HALO, :] = jnp.zeros((CONV_HALO, MIX_WIDTH), F32)

    @pl.when(s_idx != 0)
    def _():
        ext_ref[0:CONV_HALO, :] = ext_ref[TILE:TILE + CONV_HALO, :]

    def store(rows, cols, acc):
        y_ref[rows, cols] = acc + cb_ref[:, cols]

    for r0, r1, is_last in _sub_blocks():
        rows = slice(r0, r1)
        xb = x_ref[0, rows, :].astype(BF16)
        for c in range(MIX_WIDTH // 256):
            cols = slice(c * 256, (c + 1) * 256)
            a = _dot(xb, win_ref[:, c * 256:(c + 1) * 256])
            gate = _dot(xb, win_ref[:, MIX_WIDTH + c * 256:MIX_WIDTH + (c + 1) * 256])
            ext_ref[CONV_HALO + r0:CONV_HALO + r1, cols] = a * jax.nn.sigmoid(gate)
            if c > 0:
                _causal_conv(ext_ref, cw_ref, CONV_HALO, CONV_WIDTH, r0, r1, (c - 1) * 256, c * 256, store)
        qx = _dot(xb, win_ref[:, 2 * MIX_WIDTH:])
        _causal_conv(ext_ref, cw_ref, CONV_HALO, CONV_WIDTH, r0, r1, MIX_WIDTH - 256, MIX_WIDTH, store)
        br_ref[rows, MIX_WIDTH:] = _cross_attention(qx, kbd_ref, vbd_ref).astype(BF16)
        br_ref[rows, :MIX_WIDTH] = jax.nn.silu(
            _layer_norm(y_ref[rows, :], lg_ref[...], lb_ref[...])).astype(BF16)
        for e0, e1 in _epilogue_blocks(r0, r1, is_last):
            _project_and_norm(x_ref, br_ref, wo_ref, g_ref, b_ref, out_ref, e0, e1)


def _short_kernel(x_ref, win_ref, cw_ref, kbd_ref, vbd_ref, wo_ref, g_ref, b_ref,
                  out_ref, br_ref, ext_ref, bg_ref):
    s_idx = pl.program_id(1)

    @pl.when(s_idx == 0)
    def _():
        ext_ref[0:SHORT_HALO, :] = jnp.zeros((SHORT_HALO, MIX_WIDTH), F32)

    @pl.when(s_idx != 0)
    def _():
        ext_ref[0:SHORT_HALO, :] = ext_ref[TILE:TILE + SHORT_HALO, :]

    def store(rows, cols, acc):
        br_ref[rows, cols] = (bg_ref[rows, cols] * acc).astype(BF16)

    for r0, r1, is_last in _sub_blocks():
        rows = slice(r0, r1)
        xb = x_ref[0, rows, :].astype(BF16)
        for c in range(MIX_WIDTH // 256):
            cols = slice(c * 256, (c + 1) * 256)
            bg_ref[rows, cols] = _dot(xb, win_ref[:, c * 256:(c + 1) * 256])
            cg = _dot(xb, win_ref[:, MIX_WIDTH + c * 256:MIX_WIDTH + (c + 1) * 256])
            hv = _dot(xb, win_ref[:, 2 * MIX_WIDTH + c * 256:2 * MIX_WIDTH + (c + 1) * 256])
            ext_ref[SHORT_HALO + r0:SHORT_HALO + r1, cols] = cg * hv
            if c > 0:
                _causal_conv(ext_ref, cw_ref, SHORT_HALO, SHORT_CONV_WIDTH, r0, r1, (c - 1) * 256, c * 256, store)
        qx = _dot(xb, win_ref[:, 3 * MIX_WIDTH:])
        _causal_conv(ext_ref, cw_ref, SHORT_HALO, SHORT_CONV_WIDTH, r0, r1, MIX_WIDTH - 256, MIX_WIDTH, store)
        br_ref[rows, MIX_WIDTH:] = _cross_attention(qx, kbd_ref, vbd_ref).astype(BF16)
        for e0, e1 in _epilogue_blocks(r0, r1, is_last):
            _project_and_norm(x_ref, br_ref, wo_ref, g_ref, b_ref, out_ref, e0, e1)


def _ffn_kernel(x_ref, win_ref, wd_ref, g_ref, b_ref, out_ref, acc_ref):
    xb = x_ref[0].astype(BF16)
    for c in range(N_FF_CHUNKS):
        gate = _dot(xb, win_ref[:, c * FF_CHUNK:(c + 1) * FF_CHUNK])
        value = _dot(xb, win_ref[:, D_FF + c * FF_CHUNK:D_FF + (c + 1) * FF_CHUNK])
        act = (jax.nn.silu(gate) * value).astype(BF16)
        w_down = wd_ref[c * FF_CHUNK:(c + 1) * FF_CHUNK, :]
        if c == 0:
            acc_ref[...] = _dot(act, w_down)
        elif c < N_FF_CHUNKS - 1:
            acc_ref[...] += _dot(act, w_down)
        else:
            for r0, r1, is_last in _sub_blocks():
                for e0, e1 in _epilogue_blocks(r0, r1, is_last):
                    rows = slice(e0, e1)
                    y = DEEPNORM_ALPHA * x_ref[0, rows, :] + acc_ref[rows, :] + _dot(act[e0:e1], w_down)
                    out_ref[0, rows, :] = _layer_norm(y, g_ref[...], b_ref[...])


def _const_spec(shape):
    zeros = (0,) * len(shape)
    return pl.BlockSpec(shape, lambda b, s: zeros, pipeline_mode=pl.Buffered(1))


def _per_batch_spec(shape):
    zeros = (0,) * (len(shape) - 1)
    return pl.BlockSpec((1,) + tuple(shape[1:]), lambda b, s: (b,) + zeros)


class _Resident:
    def __init__(self, stacked, index):
        self.stacked = stacked
        self.index = index


def _load_as_bf16(hbm_ref, index, dst_ref, stage_ref, sem):
    n_slots, rows = stage_ref.shape[0], stage_ref.shape[1]
    n_chunks = dst_ref.shape[0] // rows

    def copy(c):
        slot = c % n_slots
        return pltpu.make_async_copy(
            hbm_ref.at[index, pl.ds(c * rows, rows), :], stage_ref.at[slot], sem.at[slot])

    for c in range(min(n_slots - 1, n_chunks)):
        copy(c).start()
    for c in range(n_chunks):
        if c + n_slots - 1 < n_chunks:
            copy(c + n_slots - 1).start()
        copy(c).wait()
        dst_ref[c * rows:(c + 1) * rows, :] = stage_ref[c % n_slots].astype(BF16)


def _stage_rows(rows, cols):
    fits = [d for d in range(BF16_SUBLANES, rows + 1, BF16_SUBLANES)
            if rows % d == 0 and d * cols <= STAGE_ELEMS]
    return max(fits)


def _token_call(body, name, x, consts, per_batch, scratch):
    bsz, seq, _ = x.shape
    x_spec = pl.BlockSpec((1, TILE, D_MODEL), lambda b, s: (b, s, 0))
    operands, specs, resident = [x], [x_spec], []

    def add_const(c):
        if isinstance(c, _Resident):
            resident.append((len(operands), c.index))
            operands.append(c.stacked)
            specs.append(pl.BlockSpec(memory_space=pl.ANY))
        else:
            operands.append(c)
            specs.append(_const_spec(c.shape))

    for c in consts[0]:
        add_const(c)
    for p in per_batch:
        operands.append(p)
        specs.append(_per_batch_spec(p.shape))
    for c in consts[1]:
        add_const(c)

    n_in, n_res = len(operands), len(resident)
    shapes = [operands[pos].shape[1:] for pos, _ in resident]
    resident_scratch = ([pltpu.VMEM(s, BF16) for s in shapes]
                        + [pltpu.VMEM((STAGE_SLOTS, _stage_rows(*s), s[1]), F32) for s in shapes]
                        + [pltpu.SemaphoreType.DMA((STAGE_SLOTS,))])

    def wrapped(*refs):
        in_refs, out_ref, extra = list(refs[:n_in]), refs[n_in], refs[n_in + 1:]
        copies, stages, sem = extra[:n_res], extra[n_res:2 * n_res], extra[2 * n_res]

        @pl.when(jnp.logical_and(pl.program_id(0) == 0, pl.program_id(1) == 0))
        def _():
            for (pos, index), dst, stage in zip(resident, copies, stages):
                _load_as_bf16(in_refs[pos], index, dst, stage, sem)

        for (pos, _), dst in zip(resident, copies):
            in_refs[pos] = dst
        body(*in_refs, out_ref, *extra[2 * n_res + 1:])

    return pl.pallas_call(
        wrapped,
        grid=(bsz, seq // TILE),
        in_specs=specs,
        out_specs=x_spec,
        out_shape=jax.ShapeDtypeStruct(x.shape, F32),
        scratch_shapes=resident_scratch + scratch,
        compiler_params=pltpu.CompilerParams(
            dimension_semantics=("arbitrary", "arbitrary"),
            vmem_limit_bytes=VMEM_LIMIT_BYTES),
        name=name,
    )(*operands)


def _row(v):
    return v.reshape(1, -1)


def kernel(x, mem, a_w_in, a_v_ln_g, a_v_ln_b, a_w_s, a_b_s, b_w_in, b_sinks, rel_bias, c_w_in, c_conv_w, c_conv_b, c_ln_g, c_ln_b, d_w_in, d_conv_w, w_mem_kv, w_o, ln1_g, ln1_b, ffn_w_in, ffn_w_down, ln2_g, ln2_b):
    kbd, vbd = _memory_kv(mem, w_mem_kv)
    branch = pltpu.VMEM((TILE, BRANCH_WIDTH), BF16)
    for i in range(DEPTH):
        m, j = i % 4, i // 4
        tail = [_Resident(w_o, i), _row(ln1_g[i]), _row(ln1_b[i])]
        kv = [kbd[i], vbd[i]]
        if m == 0:
            bias = jnp.broadcast_to(a_b_s[j][:, :, None], (SG_GROUPS, SG_BLOCK, SG_BLOCK))
            head = [_Resident(a_w_in, j), _row(a_v_ln_g[j]), _row(a_v_ln_b[j]), a_w_s[j], bias]
            scratch = [branch, pltpu.VMEM((TILE, MIX_WIDTH), F32), pltpu.VMEM((TILE, MIX_WIDTH), BF16)]
            x = _token_call(_sg_kernel, "mixer_spatial_gating", x, (head, tail), kv, scratch)
        elif m == 1:
            table = _bias_table(rel_bias, b_sinks[j]).reshape(SWA_KV_HEADS, SWA_GROUP * CHUNK, BAND_PAD)
            head = [_Resident(b_w_in, j), table]
            ext = pltpu.VMEM((SWA_KV_HEADS, TILE + BAND - CHUNK, SWA_KV_HEADS * HEAD_DIM), BF16)
            scratch = [branch, pltpu.VMEM((TILE, MIX_WIDTH), BF16), ext, ext]
            x = _token_call(_swa_kernel, "mixer_swa", x, (head, tail), kv, scratch)
        elif m == 2:
            head = [_Resident(c_w_in, j), c_conv_w[j], _row(c_conv_b[j]), _row(c_ln_g[j]), _row(c_ln_b[j])]
            scratch = [branch, pltpu.VMEM((TILE + CONV_HALO, MIX_WIDTH), F32), pltpu.VMEM((TILE, MIX_WIDTH), F32)]
            x = _token_call(_conf_kernel, "mixer_conformer", x, (head, tail), kv, scratch)
        else:
            head = [_Resident(d_w_in, j), d_conv_w[j]]
            scratch = [branch, pltpu.VMEM((TILE + SHORT_HALO, MIX_WIDTH), F32), pltpu.VMEM((TILE, MIX_WIDTH), F32)]
            x = _token_call(_short_kernel, "mixer_short_conv", x, (head, tail), kv, scratch)

        head = [_Resident(ffn_w_in, i), _Resident(ffn_w_down, i), _row(ln2_g[i]), _row(ln2_b[i])]
        x = _token_call(_ffn_kernel, "ffn_swiglu", x, (head, []), [], [pltpu.VMEM((TILE, D_MODEL), F32)])
    return x
```
